```python
import jax, jax.numpy as jnp
from jax import lax
import numpy as np

D_MODEL = 1024
BATCH = 8
SEQ = 2048
DEPTH = 4
DEC_BATCH = 128
DEC_SEQ = 1
PAST_LEN = 8192
PAGE_SIZE = 128

HEAD_DIM = 64
N_HEADS = 8
N_KV = 2
GQA_G = N_HEADS // N_KV
WINDOW = 128
BLK = WINDOW
ROPE_THETA = 10000.0
R_HEADS = 4
R_DK = 128
R_DV = 128
HGRN_CHUNK = 64
ATTN_W = N_HEADS * HEAD_DIM
KV_W = N_KV * HEAD_DIM
R_KW = R_HEADS * R_DK
R_VW = R_HEADS * R_DV
MIX_W = ATTN_W + R_VW
P_IN = ATTN_W + 2 * KV_W + 2 * R_KW + 2 * R_VW
N_MEM = 256
X_HEADS = 4
X_HEAD_DIM = D_MODEL // X_HEADS
D_FF = 2816
CONV_W = 3
EPS = 1e-6

kernel_name = 'hymba_swa_hgrn2_memxattn_convffn_step'


def rms_norm(x, g):
    x32 = x.astype(jnp.float32)
    y = x32 * lax.rsqrt(jnp.mean(x32 * x32, axis=-1, keepdims=True) + EPS)
    return (y * g.astype(jnp.float32)).astype(x.dtype)


def rope(x, pos):
    half = HEAD_DIM // 2
    inv_freq = ROPE_THETA ** (-jnp.arange(half, dtype=jnp.float32) / half)
    ang = pos.astype(jnp.float32)[:, None] * inv_freq[None, :]
    cos = jnp.cos(ang)[None, :, None, :]
    sin = jnp.sin(ang)[None, :, None, :]
    x32 = x.astype(jnp.float32)
    x1, x2 = x32[..., :half], x32[..., half:]
    return jnp.concatenate([x1 * cos - x2 * sin, x2 * cos + x1 * sin], axis=-1).astype(x.dtype)


def sink_attention(q, k, v, mask, sink):
    s = jnp.einsum('...qkgd,...skd->...kgqs', q.astype(jnp.float32), k.astype(jnp.float32)) * (HEAD_DIM ** -0.5)
    s = jnp.where(mask, s, -jnp.inf)
    sk = sink.astype(jnp.float32)[..., None, None]
    m = jnp.maximum(jnp.max(s, axis=-1, keepdims=True), sk)
    p = jnp.exp(s - m)
    p = p / (jnp.sum(p, axis=-1, keepdims=True) + jnp.exp(sk - m))
    return jnp.einsum('...kgqs,...skd->...qkgd', p, v.astype(jnp.float32))


def swa_prompt(q, k, v, sink):
    B, L = q.shape[0], q.shape[1]
    nb = L // BLK
    qb = q.reshape(B, nb, BLK, N_KV, GQA_G, HEAD_DIM)
    pad = ((0, 0), (BLK, 0), (0, 0), (0, 0))
    kp = jnp.pad(k, pad).reshape(B, nb + 1, BLK, N_KV, HEAD_DIM)
    vp = jnp.pad(v, pad).reshape(B, nb + 1, BLK, N_KV, HEAD_DIM)
    kb = jnp.concatenate([kp[:, :-1], kp[:, 1:]], axis=2)
    vb = jnp.concatenate([vp[:, :-1], vp[:, 1:]], axis=2)
    qpos = jnp.arange(nb)[:, None] * BLK + jnp.arange(BLK)[None, :]
    kpos = jnp.arange(nb)[:, None] * BLK - BLK + jnp.arange(2 * BLK)[None, :]
    diff = qpos[:, :, None] - kpos[:, None, :]
    mask = (diff >= 0) & (diff <= WINDOW) & (kpos[:, None, :] >= 0)
    mask = mask[None, :, None, None]
    o = sink_attention(qb, kb, vb, mask, sink.reshape(N_KV, GQA_G))
    return o.reshape(B, L, ATTN_W)


def swa_sample(q, k, v, buf_k, buf_v, pos, sink):
    DB, L = q.shape[0], q.shape[1]
    W = buf_k.shape[1]
    kk = jnp.concatenate([buf_k.astype(k.dtype), k], axis=1)
    vv = jnp.concatenate([buf_v.astype(v.dtype), v], axis=1)
    kpos = jnp.concatenate([pos[0] - W + jnp.arange(W, dtype=jnp.int32), pos])
    diff = pos[:, None] - kpos[None, :]
    mask = (diff >= 0) & (diff <= WINDOW)
    o = sink_attention(q.reshape(DB, L, N_KV, GQA_G, HEAD_DIM), kk, vv, mask, sink.reshape(N_KV, GQA_G))
    return o.reshape(DB, L, ATTN_W)


def hgrn_chunked(q, g, k, v, s0, chunk):
    B, L = q.shape[0], q.shape[1]
    n = L // chunk

    def to_chunks(t):
        return jnp.moveaxis(t.reshape(B, n, chunk, t.shape[2], t.shape[3]), 1, 0)

    tril = jnp.tril(jnp.ones((chunk, chunk), dtype=bool))[None, :, :, None, None]

    def step(S, inp):
        qc, gc, kc, vc = inp
        b = jnp.cumsum(gc, axis=1)
        dec = jnp.exp(jnp.where(tril, b[:, :, None] - b[:, None, :], -jnp.inf))
        att = jnp.einsum('btshk,bshk->bhts', qc[:, :, None] * dec, kc)
        o = jnp.einsum('bhts,bshv->bthv', att, vc) + jnp.einsum('bthk,bhkv->bthv', qc * jnp.exp(b), S)
        b_last = b[:, -1]
        S = jnp.exp(b_last)[..., None] * S + jnp.einsum('bshk,bshv->bhkv', kc * jnp.exp(b_last[:, None] - b), vc)
        return S, o

    S, o = lax.scan(step, s0, (to_chunks(q), to_chunks(g), to_chunks(k), to_chunks(v)))
    o = jnp.moveaxis(o, 0, 1).reshape(B, L, R_HEADS, R_DV)
    return o, S


def hgrn_branch(qr, fr, ir, gr, lb_l, gn_l, s0, chunk):
    B, L = qr.shape[0], qr.shape[1]
    lb = lb_l.reshape(R_HEADS, R_DK)
    z = fr.astype(jnp.float32).reshape(B, L, R_HEADS, R_DK)
    logf = jnp.logaddexp(jnp.log(lb), jnp.log1p(-lb) + jax.nn.log_sigmoid(z))
    k = (1.0 - lb) * jax.nn.sigmoid(-z)
    q = jax.nn.silu(qr.astype(jnp.float32)).reshape(B, L, R_HEADS, R_DK)
    v = ir.astype(jnp.float32).reshape(B, L, R_HEADS, R_DV)
    o, S = hgrn_chunked(q, logf, k, v, s0, chunk)
    o = o * lax.rsqrt(jnp.mean(o * o, axis=-1, keepdims=True) + EPS) * gn_l.astype(jnp.float32).reshape(R_HEADS, R_DV)
    o = o.reshape(B, L, R_VW) * jax.nn.silu(gr.astype(jnp.float32))
    return o, S


def token_mixer(h, pos, w_in_l, w_o_l, sink_l, lb_l, gn_l, s0, chunk, buf_k, buf_v):
    B, L = h.shape[0], h.shape[1]
    proj = h @ w_in_l
    offs = [ATTN_W, ATTN_W + KV_W, ATTN_W + 2 * KV_W, ATTN_W + 2 * KV_W + R_KW,
            ATTN_W + 2 * KV_W + 2 * R_KW, ATTN_W + 2 * KV_W + 2 * R_KW + R_VW]
    qa, ka, va, qr, fr, ir, gr = jnp.split(proj, offs, axis=-1)
    qa = rope(qa.reshape(B, L, N_HEADS, HEAD_DIM), pos)
    ka = rope(ka.reshape(B, L, N_KV, HEAD_DIM), pos)
    va = va.reshape(B, L, N_KV, HEAD_DIM)
    if buf_k is None:
        attn = swa_prompt(qa, ka, va, sink_l)
    else:
        attn = swa_sample(qa, ka, va, buf_k, buf_v, pos, sink_l)
    rec, S = hgrn_branch(qr, fr, ir, gr, lb_l, gn_l, s0, chunk)
    mixed = jnp.concatenate([attn, rec], axis=-1).astype(h.dtype) @ w_o_l
    return mixed, ka, va, S


def cross_attention(h, mk, mv, w_xq_l, w_xo_l):
    B, L = h.shape[0], h.shape[1]
    q = (h @ w_xq_l).reshape(B, L, X_HEADS, X_HEAD_DIM)
    s = jnp.einsum('bqhd,bmhd->bhqm', q.astype(jnp.float32), mk.astype(jnp.float32)) * (X_HEAD_DIM ** -0.5)
    p = jax.nn.softmax(s, axis=-1)
    o = jnp.einsum('bhqm,bmhd->bqhd', p, mv.astype(jnp.float32))
    return o.reshape(B, L, D_MODEL).astype(h.dtype) @ w_xo_l


def conv_ffn(h, w_up_l, conv_w_l, conv_b_l, w_down_l, buf):
    u = h @ w_up_l
    L = u.shape[1]
    up = jnp.concatenate([buf.astype(u.dtype), u], axis=1)
    c = conv_b_l
    for j in range(CONV_W):
        c = c + up[:, j:j + L] * conv_w_l[j]
    a, b = jnp.split(c, 2, axis=-1)
    y = (jax.nn.silu(a) * b) @ w_down_l
    return y, up[:, -(CONV_W - 1):]


def setup_inputs(seed: int = 0) -> dict:
    key = jax.random.key(seed)
    ks = jax.random.split(key, 32)
    nrm = jax.random.normal
    f32 = jnp.float32
    w_buf = min(WINDOW, PAST_LEN)

    def gain(k, shape):
        return 1.0 + 0.05 * nrm(k, shape, f32)

    return {
        'x_prompt': nrm(ks[0], (BATCH, SEQ, D_MODEL), f32),
        'x_sample': nrm(ks[1], (DEC_BATCH, DEC_SEQ, D_MODEL), f32),
        'cache_win_k': nrm(ks[2], (DEPTH, DEC_BATCH, w_buf, N_KV, HEAD_DIM), f32),
        'cache_win_v': nrm(ks[3], (DEPTH, DEC_BATCH, w_buf, N_KV, HEAD_DIM), f32),
        'cache_mem_k': nrm(ks[4], (DEPTH, DEC_BATCH, N_MEM, X_HEADS, X_HEAD_DIM), f32),
        'cache_mem_v': nrm(ks[5], (DEPTH, DEC_BATCH, N_MEM, X_HEADS, X_HEAD_DIM), f32),
        'state_hgrn': 0.3 * nrm(ks[6], (DEPTH, DEC_BATCH, R_HEADS, R_DK, R_DV), f32),
        'cache_ffn_conv': nrm(ks[7], (DEPTH, DEC_BATCH, CONV_W - 1, 2 * D_FF), f32),
        'mem_prompt': nrm(ks[8], (BATCH, N_MEM, D_MODEL), f32),
        'w_in': nrm(ks[9], (DEPTH, D_MODEL, P_IN), f32) * D_MODEL ** -0.5,
        'w_o': nrm(ks[10], (DEPTH, MIX_W, D_MODEL), f32) * MIX_W ** -0.5,
        'attn_sinks': 0.5 * nrm(ks[11], (DEPTH, N_HEADS), f32),
        'lb_logits': 0.5 * nrm(ks[12], (DEPTH, R_KW), f32),
        'hgrn_norm': gain(ks[13], (DEPTH, R_VW)),
        'w_xq': nrm(ks[14], (DEPTH, D_MODEL, D_MODEL), f32) * D_MODEL ** -0.5,
        'w_xk': nrm(ks[15], (DEPTH, D_MODEL, D_MODEL), f32) * D_MODEL ** -0.5,
        'w_xv': nrm(ks[16], (DEPTH, D_MODEL, D_MODEL), f32) * D_MODEL ** -0.5,
        'w_xo': nrm(ks[17], (DEPTH, D_MODEL, D_MODEL), f32) * D_MODEL ** -0.5,
        'w_up': nrm(ks[18], (DEPTH, D_MODEL, 2 * D_FF), f32) * D_MODEL ** -0.5,
        'conv_w': nrm(ks[19], (DEPTH, CONV_W, 2 * D_FF), f32) * CONV_W ** -0.5,
        'conv_b': 0.02 * nrm(ks[20], (DEPTH, 2 * D_FF), f32),
        'w_down': nrm(ks[21], (DEPTH, D_FF, D_MODEL), f32) * D_FF ** -0.5,
        'g_pre_mix': gain(ks[22], (DEPTH, D_MODEL)),
        'g_post_mix': gain(ks[23], (DEPTH, D_MODEL)),
        'g_pre_x': gain(ks[24], (DEPTH, D_MODEL)),
        'g_post_x': gain(ks[25], (DEPTH, D_MODEL)),
        'g_mem': gain(ks[26], (DEPTH, D_MODEL)),
        'g_pre_ffn': gain(ks[27], (DEPTH, D_MODEL)),
        'g_post_ffn': gain(ks[28], (DEPTH, D_MODEL)),
    }


def reference(x_prompt, x_sample, cache_win_k, cache_win_v, cache_mem_k, cache_mem_v, state_hgrn, cache_ffn_conv,
              mem_prompt, w_in, w_o, attn_sinks, lb_logits, hgrn_norm, w_xq, w_xk, w_xv, w_xo, w_up, conv_w, conv_b,
              w_down, g_pre_mix, g_post_mix, g_pre_x, g_post_x, g_mem, g_pre_ffn, g_post_ffn):
    c = jnp.cumsum(jax.nn.softmax(lb_logits.astype(jnp.float32), axis=0), axis=0)
    lb = c - c[0:1]

    xp, xs = x_prompt, x_sample
    Bp, Lp = xp.shape[0], xp.shape[1]
    Ls = xs.shape[1]
    pos_p = jnp.arange(Lp, dtype=jnp.int32)
    pos_s = PAST_LEN + jnp.arange(Ls, dtype=jnp.int32)
    n_keep = min(WINDOW, Lp)
    chunk_p = min(HGRN_CHUNK, Lp)

    wkp, wvp, wks, wvs, mkp, mvp, hsp, hss, cvp, cvs = [], [], [], [], [], [], [], [], [], []
    for l in range(DEPTH):
        mp, kp_, vp_, Sp = token_mixer(rms_norm(xp, g_pre_mix[l]), pos_p, w_in[l], w_o[l], attn_sinks[l], lb[l],
                                       hgrn_norm[l], jnp.zeros((Bp, R_HEADS, R_DK, R_DV), jnp.float32), chunk_p,
                                       None, None)
        xp = xp + rms_norm(mp, g_post_mix[l])
        ms, ks_, vs_, Ss = token_mixer(rms_norm(xs, g_pre_mix[l]), pos_s, w_in[l], w_o[l], attn_sinks[l], lb[l],
                                       hgrn_norm[l], state_hgrn[l].astype(jnp.float32), Ls,
                                       cache_win_k[l], cache_win_v[l])
        xs = xs + rms_norm(ms, g_post_mix[l])
        wkp.append(kp_[:, -n_keep:])
        wvp.append(vp_[:, -n_keep:])
        wks.append(ks_)
        wvs.append(vs_)
        hsp.append(Sp.astype(x_prompt.dtype))
        hss.append(Ss.astype(state_hgrn.dtype))

        hm = rms_norm(mem_prompt, g_mem[l])
        mk = (hm @ w_xk[l]).reshape(Bp, N_MEM, X_HEADS, X_HEAD_DIM)
        mv = (hm @ w_xv[l]).reshape(Bp, N_MEM, X_HEADS, X_HEAD_DIM)
        xp = xp + rms_norm(cross_attention(rms_norm(xp, g_pre_x[l]), mk, mv, w_xq[l], w_xo[l]), g_post_x[l])
        xs = xs + rms_norm(cross_attention(rms_norm(xs, g_pre_x[l]), cache_mem_k[l], cache_mem_v[l], w_xq[l],
                                           w_xo[l]), g_post_x[l])
        mkp.append(mk)
        mvp.append(mv)

        fp, bufp = conv_ffn(rms_norm(xp, g_pre_ffn[l]), w_up[l], conv_w[l], conv_b[l], w_down[l],
                            jnp.zeros((Bp, CONV_W - 1, 2 * D_FF), xp.dtype))
        xp = xp + rms_norm(fp, g_post_ffn[l])
        fs, bufs = conv_ffn(rms_norm(xs, g_pre_ffn[l]), w_up[l], conv_w[l], conv_b[l], w_down[l], cache_ffn_conv[l])
        xs = xs + rms_norm(fs, g_post_ffn[l])
        cvp.append(bufp)
        cvs.append(bufs)

    y_prompt, y_sample = xp, xs
    win_k_prompt, win_v_prompt = jnp.stack(wkp), jnp.stack(wvp)
    win_k_sample, win_v_sample = jnp.stack(wks), jnp.stack(wvs)
    mem_k_prompt, mem_v_prompt = jnp.stack(mkp), jnp.stack(mvp)
    hgrn_prompt, hgrn_sample = jnp.stack(hsp), jnp.stack(hss)
    conv_prompt, conv_sample = jnp.stack(cvp), jnp.stack(cvs)
    return (y_prompt, y_sample, win_k_prompt, win_v_prompt, win_k_sample, win_v_sample, mem_k_prompt, mem_v_prompt,
            hgrn_prompt, hgrn_sample, conv_prompt, conv_sample)
```

```python
import functools
import math

import jax
import jax.numpy as jnp
from jax import lax
from jax.experimental import pallas as pl
from jax.experimental.pallas import tpu as pltpu

F32 = jnp.float32
BF16 = jnp.bfloat16

D_MODEL = 1024
HEAD_DIM = 64
N_HEADS = 8
N_KV = 2
GQA_G = N_HEADS // N_KV
WINDOW = 128
ROPE_THETA = 10000.0
R_HEADS = 4
R_DK = 128
R_DV = 128
ATTN_W = N_HEADS * HEAD_DIM
KV_W = N_KV * HEAD_DIM
R_KW = R_HEADS * R_DK
R_VW = R_HEADS * R_DV
R_IN_W = 2 * R_KW + 2 * R_VW
P_IN = ATTN_W + 2 * KV_W + R_IN_W
N_MEM = 256
X_HEADS = 4
X_HEAD_DIM = D_MODEL // X_HEADS
D_FF = 2816
CONV_W = 3
EPS = 1e-6
PAST_LEN = 8192

LANES = 128
SUBLANES = 8
VMEM_LIMIT = 56 * 1024 * 1024

TM = 512
HC = 128
FC = 256
S_BT = 8
X_BT = 8


def _cparams(n_axes):
    return pltpu.CompilerParams(dimension_semantics=("arbitrary",) * n_axes,
                                vmem_limit_bytes=VMEM_LIMIT)


def _rms(x, g):
    return x * lax.rsqrt(jnp.mean(x * x, axis=-1, keepdims=True) + EPS) * g


def _sigmoid(x):
    return 1.0 / (1.0 + jnp.exp(-x))


def _dot(a, b):
    return jnp.dot(a, b, preferred_element_type=F32)


def _dot_nt(a, b):
    return lax.dot_general(a, b, (((1,), (1,)), ((), ())), preferred_element_type=F32)


def _dot_tn(a, b):
    return lax.dot_general(a, b, (((0,), (0,)), ((), ())), preferred_element_type=F32)


def _resident(block_shape, index_map):
    return pl.BlockSpec(block_shape, index_map, pipeline_mode=pl.Buffered(1))


def _prep_kernel(lbl_ref, lb_ref, cos_ref, sin_ref, cos_s_ref, sin_s_ref):
    x = lbl_ref[...]
    e = jnp.exp(x - jnp.max(x, axis=0, keepdims=True))
    sm = e / jnp.sum(e, axis=0, keepdims=True)
    depth = x.shape[0]
    acc = jnp.zeros((1, x.shape[1]), F32)
    rows = [acc]
    for l in range(1, depth):
        acc = acc + sm[l:l + 1, :]
        rows.append(acc)
    lb_ref[...] = jnp.concatenate(rows, axis=0)

    half = HEAD_DIM // 2

    def tables(shape, pos):
        lane = lax.broadcasted_iota(jnp.int32, shape, 1)
        j = (lane & (half - 1)).astype(F32)
        inv_freq = jnp.exp(j * (-math.log(ROPE_THETA) / half))
        ang = pos * inv_freq
        first = (lane & (HEAD_DIM - 1)) < half
        return jnp.cos(ang), jnp.where(first, -jnp.sin(ang), jnp.sin(ang))

    pos_p = lax.broadcasted_iota(jnp.int32, cos_ref.shape, 0).astype(F32)
    c, s = tables(cos_ref.shape, pos_p)
    cos_ref[...] = c
    sin_ref[...] = s
    c, s = tables(cos_s_ref.shape, jnp.full(cos_s_ref.shape, float(PAST_LEN), F32))
    cos_s_ref[...] = c
    sin_s_ref[...] = s


def _prep(lb_logits, seq, n_sample):
    depth = lb_logits.shape[0]
    out_shape = (jax.ShapeDtypeStruct((depth, R_KW), F32),
                 jax.ShapeDtypeStruct((seq, LANES), F32), jax.ShapeDtypeStruct((seq, LANES), F32),
                 jax.ShapeDtypeStruct((n_sample, LANES), F32), jax.ShapeDtypeStruct((n_sample, LANES), F32))
    return pl.pallas_call(_prep_kernel, out_shape=out_shape, name="prep")(lb_logits)


def _rope(x, cos, sin):
    w = x.shape[-1]
    reps = w // LANES
    if reps > 1:
        cos = jnp.tile(cos, (1, reps))
        sin = jnp.tile(sin, (1, reps))
    half = HEAD_DIM // 2
    lane = lax.broadcasted_iota(jnp.int32, x.shape, 1)
    first = (lane & (HEAD_DIM - 1)) < half
    swapped = jnp.where(first, pltpu.roll(x, w - half, 1), pltpu.roll(x, half, 1))
    return x * cos + swapped * sin


def _proj_in_kernel(l_ref, x_ref, g_ref, w_ref, cos_ref, sin_ref, q_ref, kv_ref, r_ref):
    h = _rms(x_ref[...], g_ref[...]).astype(BF16)
    cos = cos_ref[...]
    sin = sin_ref[...]
    q = _dot(h, w_ref[:, 0:ATTN_W])
    q_ref[...] = _rope(q, cos, sin).astype(q_ref.dtype)
    k = _dot(h, w_ref[:, ATTN_W:ATTN_W + KV_W])
    kv_ref[:, 0:KV_W] = _rope(k, cos, sin)
    kv_ref[:, KV_W:2 * KV_W] = _dot(h, w_ref[:, ATTN_W + KV_W:ATTN_W + 2 * KV_W])
    r_ref[...] = _dot(h, w_ref[:, ATTN_W + 2 * KV_W:P_IN])


def _proj_in(layer, x, g, w_in, cos, sin, tm, q_dtype):
    t = x.shape[0]
    n_tab = cos.shape[0] // tm
    grid_spec = pltpu.PrefetchScalarGridSpec(
        num_scalar_prefetch=1, grid=(t // tm,),
        in_specs=[
            pl.BlockSpec((tm, D_MODEL), lambda i, l: (i, 0)),
            pl.BlockSpec((1, D_MODEL), lambda i, l: (0, 0)),
            _resident((None, D_MODEL, P_IN), lambda i, l: (l[0], 0, 0)),
            pl.BlockSpec((tm, LANES), lambda i, l: (i % n_tab, 0)),
            pl.BlockSpec((tm, LANES), lambda i, l: (i % n_tab, 0)),
        ],
        out_specs=[
            pl.BlockSpec((tm, ATTN_W), lambda i, l: (i, 0)),
            pl.BlockSpec((tm, 2 * KV_W), lambda i, l: (i, 0)),
            pl.BlockSpec((tm, R_IN_W), lambda i, l: (i, 0)),
        ])
    out_shape = (jax.ShapeDtypeStruct((t, ATTN_W), q_dtype),
                 jax.ShapeDtypeStruct((t, 2 * KV_W), F32),
                 jax.ShapeDtypeStruct((t, R_IN_W), F32))
    return pl.pallas_call(_proj_in_kernel, grid_spec=grid_spec, out_shape=out_shape,
                          compiler_params=_cparams(1), name="proj_in")(layer, x, g, w_in, cos, sin)


def _swa_kernel(q_ref, kvc_ref, kvp_ref, sink_ref, o_ref):
    j = pl.program_id(1)
    q = q_ref[...]
    kvc = kvc_ref[...]
    kvp = kvp_ref[...]
    row = lax.broadcasted_iota(jnp.int32, (WINDOW, 2 * WINDOW), 0)
    col = lax.broadcasted_iota(jnp.int32, (WINDOW, 2 * WINDOW), 1)
    first_key = jnp.where(j > 0, 0, WINDOW)
    valid = (col >= row) & (col <= row + WINDOW) & (col >= first_key)
    scale = HEAD_DIM ** -0.5
    for g in range(N_KV):
        ks = slice(g * HEAD_DIM, (g + 1) * HEAD_DIM)
        vs = slice(KV_W + g * HEAD_DIM, KV_W + (g + 1) * HEAD_DIM)
        k = jnp.concatenate([kvp[:, ks], kvc[:, ks]], axis=0).astype(BF16)
        v = jnp.concatenate([kvp[:, vs], kvc[:, vs]], axis=0).astype(BF16)
        for hh in range(GQA_G):
            h = g * GQA_G + hh
            hs = slice(h * HEAD_DIM, (h + 1) * HEAD_DIM)
            s = _dot_nt(q[:, hs], k) * scale
            s = jnp.where(valid, s, -jnp.inf)
            sk = sink_ref[0:1, h:h + 1]
            m = jnp.maximum(jnp.max(s, axis=-1, keepdims=True), sk)
            p = jnp.exp(s - m)
            den = jnp.sum(p, axis=-1, keepdims=True) + jnp.exp(sk - m)
            o = _dot(p.astype(BF16), v) / den
            o_ref[:, hs] = o.astype(o_ref.dtype)


def _swa(q, kv, sink, batch, seq):
    nb = seq // WINDOW
    return pl.pallas_call(
        _swa_kernel, grid=(batch, nb),
        in_specs=[
            pl.BlockSpec((WINDOW, ATTN_W), lambda b, j: (b * nb + j, 0)),
            pl.BlockSpec((WINDOW, 2 * KV_W), lambda b, j: (b * nb + j, 0)),
            pl.BlockSpec((WINDOW, 2 * KV_W), lambda b, j: (b * nb + jnp.maximum(j - 1, 0), 0)),
            pl.BlockSpec((1, N_HEADS), lambda b, j: (0, 0)),
        ],
        out_specs=pl.BlockSpec((WINDOW, ATTN_W), lambda b, j: (b * nb + j, 0)),
        out_shape=jax.ShapeDtypeStruct(q.shape, BF16),
        compiler_params=_cparams(2), name="swa")(q, kv, kv, sink)


def _log_forget(z, lb):
    logsig = jnp.minimum(z, 0.0) - jnp.log1p(jnp.exp(-jnp.abs(z)))
    a = jnp.log(lb)
    c = jnp.log1p(-lb) + logsig
    return jnp.maximum(a, c) + jnp.log1p(jnp.exp(-jnp.abs(a - c)))


def _cumsum_rows(g):
    n = g.shape[0]
    row = lax.broadcasted_iota(jnp.int32, g.shape, 0)
    b = g
    sh = 1
    while sh < n:
        if sh % SUBLANES == 0:
            shifted = jnp.concatenate([jnp.zeros((sh, g.shape[1]), F32), b[:n - sh]], axis=0)
        else:
            shifted = jnp.where(row >= sh, pltpu.roll(b, sh, 0), 0.0)
        b = b + shifted
        sh *= 2
    return b


def _level_ref(b, half):
    n_rows, width = b.shape
    n = 2 * half
    if n >= 2 * SUBLANES:
        pieces = [jnp.broadcast_to(b[i * n + half - 1:i * n + half, :], (n, width))
                  for i in range(n_rows // n)]
        return pieces[0] if len(pieces) == 1 else jnp.concatenate(pieces, axis=0)
    b3 = b.reshape(n_rows // SUBLANES, SUBLANES, width)
    sub = lax.broadcasted_iota(jnp.int32, b3.shape, 1)

    def bcast(r):
        return jnp.broadcast_to(b3[:, r:r + 1, :], b3.shape)

    if half == 4:
        ref = bcast(3)
    elif half == 2:
        ref = jnp.where(sub < 4, bcast(1), bcast(5))
    else:
        ref = jnp.where(sub < 2, bcast(0), jnp.where(sub < 4, bcast(2), jnp.where(sub < 6, bcast(4), bcast(6))))
    return ref.reshape(n_rows, width)


def _hgrn_kernel(qr_ref, fr_ref, ir_ref, gr_ref, lb_ref, gn_ref, o_ref, s_out_ref, st_ref):
    c = pl.program_id(1)
    last = pl.num_programs(1) - 1

    @pl.when(c == 0)
    def _():
        st_ref[...] = jnp.zeros_like(st_ref)

    rr = lax.broadcasted_iota(jnp.int32, (HC, HC), 0)
    cc = lax.broadcasted_iota(jnp.int32, (HC, HC), 1)
    xor = rr ^ cc
    causal = cc <= rr
    for h in range(R_HEADS):
        sl = slice(h * R_DK, (h + 1) * R_DK)
        z = fr_ref[:, sl]
        lb = lb_ref[:, sl]
        g = _log_forget(z, lb)
        k = (1.0 - lb) * _sigmoid(-z)
        qr = qr_ref[:, sl]
        q = qr * _sigmoid(qr)
        v = ir_ref[:, sl].astype(BF16)
        b = _cumsum_rows(g)
        att = _dot_nt(q.astype(BF16), k.astype(BF16))
        half = 1
        while half < HC:
            e = jnp.exp(-jnp.abs(b - _level_ref(b, half)))
            a = _dot_nt((q * e).astype(BF16), (k * e).astype(BF16))
            att = jnp.where(xor >= half, a, att)
            half *= 2
        att = jnp.where(causal, att, 0.0)
        st = st_ref[h]
        o = _dot(att.astype(BF16), v) + _dot_nt((q * jnp.exp(b)).astype(BF16), st.astype(BF16))
        b_last = b[HC - 1:HC, :]
        kd = (k * jnp.exp(b_last - b)).astype(BF16)
        st_new = st * jnp.exp(b_last) + _dot_tn(v, kd)
        st_ref[h] = st_new
        on = o * lax.rsqrt(jnp.mean(o * o, axis=-1, keepdims=True) + EPS) * gn_ref[:, sl]
        gr = gr_ref[:, sl]
        o_ref[:, sl] = (on * (gr * _sigmoid(gr))).astype(o_ref.dtype)

        @pl.when(c == last)
        def _():
            s_out_ref[0, h] = st_new.T


def _hgrn(r, lb, gn, batch, seq):
    nc = seq // HC
    t = r.shape[0]

    def col(kk):
        return pl.BlockSpec((HC, R_KW), lambda b, c: (b * nc + c, kk))

    return pl.pallas_call(
        _hgrn_kernel, grid=(batch, nc),
        in_specs=[col(0), col(1), col(2), col(3),
                  pl.BlockSpec((1, R_KW), lambda b, c: (0, 0)),
                  pl.BlockSpec((1, R_VW), lambda b, c: (0, 0))],
        out_specs=[pl.BlockSpec((HC, R_VW), lambda b, c: (b * nc + c, 0)),
                   pl.BlockSpec((1, R_HEADS, R_DK, R_DV), lambda b, c: (b, 0, 0, 0))],
        out_shape=(jax.ShapeDtypeStruct((t, R_VW), BF16),
                   jax.ShapeDtypeStruct((batch, R_HEADS, R_DK, R_DV), F32)),
        scratch_shapes=[pltpu.VMEM((R_HEADS, R_DV, R_DK), F32)],
        compiler_params=_cparams(2), name="hgrn")(r, r, r, r, lb, gn)


def _mix_out_kernel(l_ref, a_ref, r_ref, w_ref, x_ref, g_ref, o_ref):
    m = (_dot(a_ref[...].astype(BF16), w_ref[0:ATTN_W, :])
         + _dot(r_ref[...].astype(BF16), w_ref[ATTN_W:ATTN_W + R_VW, :]))
    o_ref[...] = x_ref[...] + _rms(m, g_ref[...])


def _mix_out(layer, attn, rec, w_o, x, g, tm):
    t = x.shape[0]
    grid_spec = pltpu.PrefetchScalarGridSpec(
        num_scalar_prefetch=1, grid=(t // tm,),
        in_specs=[
            pl.BlockSpec((tm, ATTN_W), lambda i, l: (i, 0)),
            pl.BlockSpec((tm, R_VW), lambda i, l: (i, 0)),
            _resident((None, ATTN_W + R_VW, D_MODEL), lambda i, l: (l[0], 0, 0)),
            pl.BlockSpec((tm, D_MODEL), lambda i, l: (i, 0)),
            pl.BlockSpec((1, D_MODEL), lambda i, l: (0, 0)),
        ],
        out_specs=pl.BlockSpec((tm, D_MODEL), lambda i, l: (i, 0)))
    return pl.pallas_call(_mix_out_kernel, grid_spec=grid_spec,
                          out_shape=jax.ShapeDtypeStruct(x.shape, F32),
                          compiler_params=_cparams(1), name="mix_out")(layer, attn, rec, w_o, x, g)


def _memkv_kernel(m_ref, g_ref, wk_ref, wv_ref, k_ref, v_ref):
    h = _rms(m_ref[...], g_ref[...]).astype(BF16)
    k_ref[...] = _dot(h, wk_ref[...])
    v_ref[...] = _dot(h, wv_ref[...])


def _memkv(mem, g_mem, w_xk, w_xv):
    depth = w_xk.shape[0]
    rows = mem.shape[0]
    tm = min(TM, rows)
    out = jax.ShapeDtypeStruct((depth, rows, D_MODEL), F32)
    return pl.pallas_call(
        _memkv_kernel, grid=(depth, rows // tm),
        in_specs=[
            pl.BlockSpec((tm, D_MODEL), lambda l, i: (i, 0)),
            pl.BlockSpec((None, 1, D_MODEL), lambda l, i: (l, 0, 0)),
            pl.BlockSpec((None, D_MODEL, D_MODEL), lambda l, i: (l, 0, 0)),
            pl.BlockSpec((None, D_MODEL, D_MODEL), lambda l, i: (l, 0, 0)),
        ],
        out_specs=[pl.BlockSpec((None, tm, D_MODEL), lambda l, i: (l, i, 0)),
                   pl.BlockSpec((None, tm, D_MODEL), lambda l, i: (l, i, 0))],
        out_shape=(out, out), compiler_params=_cparams(2), name="memkv")(mem, g_mem, w_xk, w_xv)


def _xattn_kernel(l_ref, x_ref, gpre_ref, wq_ref, mk_ref, mv_ref, wo_ref, gpost_ref, o_ref):
    x = x_ref[...]
    h = _rms(x, gpre_ref[...]).astype(BF16)
    q = (_dot(h, wq_ref[...]) * (X_HEAD_DIM ** -0.5)).astype(BF16)
    outs = []
    for hd in range(X_HEADS):
        sl = slice(hd * X_HEAD_DIM, (hd + 1) * X_HEAD_DIM)
        s = _dot_nt(q[:, sl], mk_ref[:, sl].astype(BF16))
        p = jnp.exp(s - jnp.max(s, axis=-1, keepdims=True))
        den = jnp.sum(p, axis=-1, keepdims=True)
        outs.append((_dot(p.astype(BF16), mv_ref[:, sl].astype(BF16)) / den).astype(BF16))
    y = _dot(jnp.concatenate(outs, axis=1), wo_ref[...])
    o_ref[...] = x + _rms(y, gpost_ref[...])


def _xattn(layer, x, gpre, w_xq, mk, mv, w_xo, gpost, seq, tm):
    t = x.shape[0]
    tpb = seq // tm
    grid_spec = pltpu.PrefetchScalarGridSpec(
        num_scalar_prefetch=1, grid=(t // tm,),
        in_specs=[
            pl.BlockSpec((tm, D_MODEL), lambda i, l: (i, 0)),
            pl.BlockSpec((1, D_MODEL), lambda i, l: (0, 0)),
            _resident((None, D_MODEL, D_MODEL), lambda i, l: (l[0], 0, 0)),
            pl.BlockSpec((None, N_MEM, D_MODEL), lambda i, l: (l[0], i // tpb, 0)),
            pl.BlockSpec((None, N_MEM, D_MODEL), lambda i, l: (l[0], i // tpb, 0)),
            _resident((None, D_MODEL, D_MODEL), lambda i, l: (l[0], 0, 0)),
            pl.BlockSpec((1, D_MODEL), lambda i, l: (0, 0)),
        ],
        out_specs=pl.BlockSpec((tm, D_MODEL), lambda i, l: (i, 0)))
    return pl.pallas_call(_xattn_kernel, grid_spec=grid_spec,
                          out_shape=jax.ShapeDtypeStruct(x.shape, F32),
                          compiler_params=_cparams(1), name="xattn")(layer, x, gpre, w_xq, mk, mv, w_xo, gpost)


def _ffn_kernel(*refs, sample, tiles_per_batch):
    if sample:
        (l_ref, x_ref, gpre_ref, wup_ref, cw_ref, cb_ref, wdn_ref, gpost_ref, buf_ref,
         o_ref, tail_ref) = refs
        carry_ref = None
    else:
        (l_ref, x_ref, gpre_ref, wup_ref, cw_ref, cb_ref, wdn_ref, gpost_ref,
         o_ref, tail_ref, carry_ref) = refs
        i = pl.program_id(0)

        @pl.when(i % tiles_per_batch == 0)
        def _():
            carry_ref[...] = jnp.zeros_like(carry_ref)

    x = x_ref[...]
    tm = x.shape[0]
    h = _rms(x, gpre_ref[...]).astype(BF16)
    row = lax.broadcasted_iota(jnp.int32, (tm, FC), 0)

    def conv(u, cs):
        w0 = cw_ref[0:1, cs]
        w1 = cw_ref[1:2, cs]
        w2 = cw_ref[2:3, cs]
        if sample:
            u2 = buf_ref[:, cs]
            u1 = buf_ref[:, slice(2 * D_FF + cs.start, 2 * D_FF + cs.stop)]
            tail_ref[:, cs] = u1
            tail_ref[:, slice(2 * D_FF + cs.start, 2 * D_FF + cs.stop)] = u
        else:
            p0 = carry_ref[0:1, cs]
            p1 = carry_ref[1:2, cs]
            u1 = jnp.where(row >= 1, pltpu.roll(u, 1, 0), p1)
            u2 = jnp.where(row >= 2, pltpu.roll(u, 2, 0), jnp.where(row == 1, p1, p0))
            carry_ref[:, cs] = u[tm - (CONV_W - 1):tm, :]
        return cb_ref[:, cs] + u2 * w0 + u1 * w1 + u * w2

    acc = jnp.zeros((tm, D_MODEL), F32)
    for j in range(D_FF // FC):
        ca_cols = slice(j * FC, (j + 1) * FC)
        cb_cols = slice(D_FF + j * FC, D_FF + (j + 1) * FC)
        ca = conv(_dot(h, wup_ref[:, ca_cols]), ca_cols)
        cb = conv(_dot(h, wup_ref[:, cb_cols]), cb_cols)
        act = (ca * _sigmoid(ca) * cb).astype(BF16)
        acc = acc + _dot(act, wdn_ref[j * FC:(j + 1) * FC, :])
    o_ref[...] = x + _rms(acc, gpost_ref[...])

    if not sample:
        @pl.when(i % tiles_per_batch == tiles_per_batch - 1)
        def _():
            tail_ref[0] = carry_ref[...]


def _ffn(layer, x, gpre, w_up, conv_w, conv_b, w_down, gpost, *, seq=None, tm=None, buf=None):
    t = x.shape[0]
    sample = buf is not None
    in_specs = [
        pl.BlockSpec((tm, D_MODEL), lambda i, l: (i, 0)),
        pl.BlockSpec((1, D_MODEL), lambda i, l: (0, 0)),
        _resident((None, D_MODEL, 2 * D_FF), lambda i, l: (l[0], 0, 0)),
        pl.BlockSpec((CONV_W, 2 * D_FF), lambda i, l: (0, 0)),
        pl.BlockSpec((1, 2 * D_FF), lambda i, l: (0, 0)),
        _resident((None, D_FF, D_MODEL), lambda i, l: (l[0], 0, 0)),
        pl.BlockSpec((1, D_MODEL), lambda i, l: (0, 0)),
    ]
    args = [layer, x, gpre, w_up, conv_w, conv_b, w_down, gpost]
    tail_w = (CONV_W - 1) * 2 * D_FF
    if sample:
        tiles_per_batch = 1
        in_specs.append(pl.BlockSpec((None, tm, tail_w), lambda i, l: (l[0], i, 0)))
        args.append(buf)
        tail_spec = pl.BlockSpec((tm, tail_w), lambda i, l: (i, 0))
        tail_shape = jax.ShapeDtypeStruct((t, tail_w), F32)
        scratch = []
    else:
        tiles_per_batch = seq // tm
        batch = t // seq
        tail_spec = pl.BlockSpec((1, CONV_W - 1, 2 * D_FF), lambda i, l: (i // tiles_per_batch, 0, 0))
        tail_shape = jax.ShapeDtypeStruct((batch, CONV_W - 1, 2 * D_FF), F32)
        scratch = [pltpu.VMEM((CONV_W - 1, 2 * D_FF), F32)]
    grid_spec = pltpu.PrefetchScalarGridSpec(
        num_scalar_prefetch=1, grid=(t // tm,), in_specs=in_specs,
        out_specs=[pl.BlockSpec((tm, D_MODEL), lambda i, l: (i, 0)), tail_spec],
        scratch_shapes=scratch)
    body = functools.partial(_ffn_kernel, sample=sample, tiles_per_batch=tiles_per_batch)
    return pl.pallas_call(body, grid_spec=grid_spec,
                          out_shape=(jax.ShapeDtypeStruct(x.shape, F32), tail_shape),
                          compiler_params=_cparams(1),
                          name="ffn_sample" if sample else "ffn")(*args)


def _s_attn_kernel(l_ref, q_ref, kv_ref, ck_ref, cv_ref, sink_ref, o_ref):
    bt = q_ref.shape[0]
    q = q_ref[...].astype(F32) * (HEAD_DIM ** -0.5)
    lane = lax.broadcasted_iota(jnp.int32, (bt, LANES), 1)
    low = lane < HEAD_DIM
    qh = []
    for h in range(N_HEADS):
        pair = q[:, (h // 2) * LANES:(h // 2 + 1) * LANES]
        g = h // GQA_G
        in_place = (h % 2) == g
        src = pair if in_place else pltpu.roll(pair, HEAD_DIM, 1)
        qh.append(jnp.where(low if g == 0 else jnp.logical_not(low), src, 0.0))

    low1 = lax.broadcasted_iota(jnp.int32, (1, LANES), 1) < HEAD_DIM

    for bi in range(bt):
        kc = ck_ref[bi]
        vc = cv_ref[bi]
        k_new = kv_ref[bi:bi + 1, 0:KV_W]
        v_new = kv_ref[bi:bi + 1, KV_W:2 * KV_W]
        for pr in range(N_HEADS // 2):
            res = []
            for h in (2 * pr, 2 * pr + 1):
                qrow = qh[h][bi:bi + 1, :]
                s = jnp.sum(kc * qrow, axis=-1, keepdims=True)
                s_new = jnp.sum(k_new * qrow, axis=-1, keepdims=True)
                sk = sink_ref[0:1, h:h + 1]
                m = jnp.maximum(jnp.maximum(jnp.max(s, axis=0, keepdims=True), s_new), sk)
                p = jnp.exp(s - m)
                p_new = jnp.exp(s_new - m)
                den = jnp.sum(p, axis=0, keepdims=True) + p_new + jnp.exp(sk - m)
                res.append((jnp.sum(p * vc, axis=0, keepdims=True) + p_new * v_new) / den)
            g = (2 * pr) // GQA_G
            if g == 0:
                pair = jnp.where(low1, res[0], pltpu.roll(res[1], HEAD_DIM, 1))
            else:
                pair = jnp.where(low1, pltpu.roll(res[0], HEAD_DIM, 1), res[1])
            o_ref[bi:bi + 1, pr * LANES:(pr + 1) * LANES] = pair


def _s_attn(layer, q, kv, cache_k, cache_v, sink):
    n = q.shape[0]
    w = cache_k.shape[2]
    grid_spec = pltpu.PrefetchScalarGridSpec(
        num_scalar_prefetch=1, grid=(n // S_BT,),
        in_specs=[
            pl.BlockSpec((S_BT, ATTN_W), lambda i, l: (i, 0)),
            pl.BlockSpec((S_BT, 2 * KV_W), lambda i, l: (i, 0)),
            pl.BlockSpec((None, S_BT, w, KV_W), lambda i, l: (l[0], i, 0, 0)),
            pl.BlockSpec((None, S_BT, w, KV_W), lambda i, l: (l[0], i, 0, 0)),
            pl.BlockSpec((1, N_HEADS), lambda i, l: (0, 0)),
        ],
        out_specs=pl.BlockSpec((S_BT, ATTN_W), lambda i, l: (i, 0)))
    return pl.pallas_call(_s_attn_kernel, grid_spec=grid_spec,
                          out_shape=jax.ShapeDtypeStruct((n, ATTN_W), F32),
                          compiler_params=_cparams(1), name="s_attn")(layer, q, kv, cache_k, cache_v, sink)


def _s_hgrn_kernel(l_ref, r_ref, rt_ref, st_ref, lbt_ref, gn_ref, o_ref, so_ref):
    bt = r_ref.shape[0]
    for h in range(R_HEADS):
        sl = slice(h * R_DK, (h + 1) * R_DK)
        zt = rt_ref[R_KW + h * R_DK:R_KW + (h + 1) * R_DK, :]
        lbc = lbt_ref[sl, :]
        f = jnp.exp(_log_forget(zt, lbc))
        kt = (1.0 - lbc) * _sigmoid(-zt)
        qrt = rt_ref[sl, :]
        qt = qrt * _sigmoid(qrt)
        for bi in range(bt):
            s = st_ref[bi, h]
            v = r_ref[bi:bi + 1, 2 * R_KW + h * R_DV:2 * R_KW + (h + 1) * R_DV]
            s_new = f[:, bi:bi + 1] * s + kt[:, bi:bi + 1] * v
            so_ref[bi, h] = s_new
            o = jnp.sum(qt[:, bi:bi + 1] * s_new, axis=0, keepdims=True)
            on = o * lax.rsqrt(jnp.mean(o * o, axis=-1, keepdims=True) + EPS) * gn_ref[:, sl]
            gr = r_ref[bi:bi + 1, 2 * R_KW + R_VW + h * R_DV:2 * R_KW + R_VW + (h + 1) * R_DV]
            o_ref[bi:bi + 1, sl] = (on * (gr * _sigmoid(gr))).astype(o_ref.dtype)


def _s_hgrn(layer, r, state, lb_col, gn):
    n = r.shape[0]
    rt = r.reshape(n // S_BT, S_BT, R_IN_W).transpose(0, 2, 1)
    grid_spec = pltpu.PrefetchScalarGridSpec(
        num_scalar_prefetch=1, grid=(n // S_BT,),
        in_specs=[
            pl.BlockSpec((S_BT, R_IN_W), lambda i, l: (i, 0)),
            pl.BlockSpec((None, R_IN_W, S_BT), lambda i, l: (i, 0, 0)),
            pl.BlockSpec((None, S_BT, R_HEADS, R_DK, R_DV), lambda i, l: (l[0], i, 0, 0, 0)),
            pl.BlockSpec((R_KW, 1), lambda i, l: (0, 0)),
            pl.BlockSpec((1, R_VW), lambda i, l: (0, 0)),
        ],
        out_specs=[pl.BlockSpec((S_BT, R_VW), lambda i, l: (i, 0)),
                   pl.BlockSpec((S_BT, R_HEADS, R_DK, R_DV), lambda i, l: (i, 0, 0, 0))])
    return pl.pallas_call(_s_hgrn_kernel, grid_spec=grid_spec,
                          out_shape=(jax.ShapeDtypeStruct((n, R_VW), F32),
                                     jax.ShapeDtypeStruct((n, R_HEADS, R_DK, R_DV), F32)),
                          compiler_params=_cparams(1), name="s_hgrn")(layer, r, rt, state, lb_col, gn)


def _s_xattn_kernel(l_ref, x_ref, gpre_ref, wq_ref, ck_ref, cv_ref, wo_ref, gpost_ref, o_ref):
    bt = ck_ref.shape[0]
    x = x_ref[...]
    h = _rms(x, gpre_ref[...]).astype(BF16)
    q = _dot(h, wq_ref[...]) * (X_HEAD_DIM ** -0.5)
    rows = []
    for bi in range(bt):
        prod = ck_ref[bi] * q[bi:bi + 1, :]
        outs = []
        for hd in range(X_HEADS):
            sl = slice(hd * X_HEAD_DIM, (hd + 1) * X_HEAD_DIM)
            s = jnp.sum(prod[:, sl], axis=-1, keepdims=True)
            p = jnp.exp(s - jnp.max(s, axis=0, keepdims=True))
            den = jnp.sum(p, axis=0, keepdims=True)
            outs.append(jnp.sum(p * cv_ref[bi, :, sl], axis=0, keepdims=True) / den)
        rows.append(jnp.concatenate(outs, axis=1))
    y = _dot(jnp.concatenate(rows, axis=0).astype(BF16), wo_ref[...])
    o_ref[...] = x + _rms(y, gpost_ref[...])


def _s_xattn(layer, x, gpre, w_xq, cache_k, cache_v, w_xo, gpost):
    n = x.shape[0]
    grid_spec = pltpu.PrefetchScalarGridSpec(
        num_scalar_prefetch=1, grid=(n // X_BT,),
        in_specs=[
            pl.BlockSpec((X_BT, D_MODEL), lambda i, l: (i, 0)),
            pl.BlockSpec((1, D_MODEL), lambda i, l: (0, 0)),
            _resident((None, D_MODEL, D_MODEL), lambda i, l: (l[0], 0, 0)),
            pl.BlockSpec((None, X_BT, N_MEM, D_MODEL), lambda i, l: (l[0], i, 0, 0)),
            pl.BlockSpec((None, X_BT, N_MEM, D_MODEL), lambda i, l: (l[0], i, 0, 0)),
            _resident((None, D_MODEL, D_MODEL), lambda i, l: (l[0], 0, 0)),
            pl.BlockSpec((1, D_MODEL), lambda i, l: (0, 0)),
        ],
        out_specs=pl.BlockSpec((X_BT, D_MODEL), lambda i, l: (i, 0)))
    return pl.pallas_call(_s_xattn_kernel, grid_spec=grid_spec,
                          out_shape=jax.ShapeDtypeStruct(x.shape, F32),
                          compiler_params=_cparams(1), name="s_xattn")(
                              layer, x, gpre, w_xq, cache_k, cache_v, w_xo, gpost)


def kernel(x_prompt, x_sample, cache_win_k, cache_win_v, cache_mem_k, cache_mem_v, state_hgrn, cache_ffn_conv, mem_prompt, w_in, w_o, attn_sinks, lb_logits, hgrn_norm, w_xq, w_xk, w_xv, w_xo, w_up, conv_w, conv_b, w_down, g_pre_mix, g_post_mix, g_pre_x, g_post_x, g_mem, g_pre_ffn, g_post_ffn):
    batch, seq, _ = x_prompt.shape
    n_s, dec_seq, _ = x_sample.shape
    depth = w_in.shape[0]
    assert dec_seq == 1 and seq % WINDOW == 0 and seq % HC == 0
    assert cache_win_k.shape[2] == WINDOW and n_s % S_BT == 0 and n_s % X_BT == 0
    tm = min(TM, seq)
    assert seq % tm == 0

    w_in_b, w_o_b, w_xq_b, w_xk_b, w_xv_b, w_xo_b, w_up_b, w_down_b = (
        w.astype(BF16) for w in (w_in, w_o, w_xq, w_xk, w_xv, w_xo, w_up, w_down))
    lb, cos_p, sin_p, cos_s, sin_s = _prep(lb_logits, seq, n_s)
    mk, mv = _memkv(mem_prompt.reshape(batch * N_MEM, D_MODEL), g_mem.reshape(depth, 1, D_MODEL),
                    w_xk_b, w_xv_b)

    cwk = cache_win_k.reshape(depth, n_s, WINDOW, KV_W)
    cwv = cache_win_v.reshape(depth, n_s, WINDOW, KV_W)
    cmk = cache_mem_k.reshape(depth, n_s, N_MEM, D_MODEL)
    cmv = cache_mem_v.reshape(depth, n_s, N_MEM, D_MODEL)
    cfc = cache_ffn_conv.reshape(depth, n_s, (CONV_W - 1) * 2 * D_FF)

    def layer_fn(carry, l):
        xp, xs = carry
        layer = jnp.reshape(l, (1,)).astype(jnp.int32)

        def vec(a):
            return lax.dynamic_index_in_dim(a, l, 0, keepdims=True)

        sink, lb_l, gn_l = vec(attn_sinks), vec(lb), vec(hgrn_norm)
        cw_l, cb_l = lax.dynamic_index_in_dim(conv_w, l, 0, keepdims=False), vec(conv_b)
        gpm, gom, gpx, gox, gpf, gof = (vec(g) for g in (g_pre_mix, g_post_mix, g_pre_x, g_post_x,
                                                        g_pre_ffn, g_post_ffn))

        q_p, kv_p, r_p = _proj_in(layer, xp, gpm, w_in_b, cos_p, sin_p, tm, BF16)
        attn_p = _swa(q_p, kv_p, sink, batch, seq)
        rec_p, st_p = _hgrn(r_p, lb_l, gn_l, batch, seq)
        xp = _mix_out(layer, attn_p, rec_p, w_o_b, xp, gom, tm)

        q_s, kv_s, r_s = _proj_in(layer, xs, gpm, w_in_b, cos_s, sin_s, n_s, F32)
        attn_s = _s_attn(layer, q_s, kv_s, cwk, cwv, sink)
        rec_s, st_s = _s_hgrn(layer, r_s, state_hgrn, lb_l.reshape(R_KW, 1), gn_l)
        xs = _mix_out(layer, attn_s, rec_s, w_o_b, xs, gom, n_s)

        xp = _xattn(layer, xp, gpx, w_xq_b, mk, mv, w_xo_b, gox, seq, tm)
        xs = _s_xattn(layer, xs, gpx, w_xq_b, cmk, cmv, w_xo_b, gox)

        xp, tail_p = _ffn(layer, xp, gpf, w_up_b, cw_l, cb_l, w_down_b, gof, seq=seq, tm=tm)
        xs, tail_s = _ffn(layer, xs, gpf, w_up_b, cw_l, cb_l, w_down_b, gof, tm=n_s, buf=cfc)

        kv_keep = kv_p.reshape(batch, seq, 2 * KV_W)[:, seq - WINDOW:, :]
        ys = (kv_keep[..., :KV_W], kv_keep[..., KV_W:], kv_s[:, :KV_W], kv_s[:, KV_W:],
              st_p, st_s, tail_p, tail_s)
        return (xp, xs), ys

    (xp, xs), ys = lax.scan(layer_fn, (x_prompt.reshape(batch * seq, D_MODEL), x_sample.reshape(n_s, D_MODEL)),
                            jnp.arange(depth, dtype=jnp.int32))
    wk_p, wv_p, wk_s, wv_s, st_p, st_s, tail_p, tail_s = ys
    return (xp.reshape(batch, seq, D_MODEL), xs.reshape(n_s, 1, D_MODEL),
            wk_p.reshape(depth, batch, WINDOW, N_KV, HEAD_DIM), wv_p.reshape(depth, batch, WINDOW, N_KV, HEAD_DIM),
            wk_s.reshape(depth, n_s, 1, N_KV, HEAD_DIM), wv_s.reshape(depth, n_s, 1, N_KV, HEAD_DIM),
            mk.reshape(depth, batch, N_MEM, X_HEADS, X_HEAD_DIM), mv.reshape(depth, batch, N_MEM, X_HEADS, X_HEAD_DIM),
            st_p, st_s,
            tail_p, tail_s.reshape(depth, n_s, CONV_W - 1, 2 * D_FF))
```

```python
import functools
import math

import jax
import jax.numpy as jnp
from jax import lax
from jax.experimental import pallas as pl
from jax.experimental.pallas import tpu as pltpu

F32 = jnp.float32
BF16 = jnp.bfloat16

D_MODEL = 1024
HEAD_DIM = 64
N_HEADS = 8
N_KV = 2
GQA_G = N_HEADS // N_KV
WINDOW = 128
ROPE_THETA = 10000.0
R_HEADS = 4
R_DK = 128
R_DV = 128
ATTN_W = N_HEADS * HEAD_DIM
KV_W = N_KV * HEAD_DIM
R_KW = R_HEADS * R_DK
R_VW = R_HEADS * R_DV
R_IN_W = 2 * R_KW + 2 * R_VW
P_IN = ATTN_W + 2 * KV_W + R_IN_W
N_MEM = 256
X_HEADS = 4
X_HEAD_DIM = D_MODEL // X_HEADS
D_FF = 2816
CONV_W = 3
EPS = 1e-6
PAST_LEN = 8192

LANES = 128
SUBLANES = 8
VMEM_LIMIT = 56 * 1024 * 1024

TM = 512
SWA_TQ = 512
HC = 128
FC = 256
S_BT = 8
X_BT = 8


def _cparams(n_axes):
    return pltpu.CompilerParams(dimension_semantics=("arbitrary",) * n_axes,
                                vmem_limit_bytes=VMEM_LIMIT)


def _rms(x, g):
    return x * lax.rsqrt(jnp.mean(x * x, axis=-1, keepdims=True) + EPS) * g


def _sigmoid(x):
    return 1.0 / (1.0 + jnp.exp(-x))


def _dot(a, b):
    return jnp.dot(a, b, preferred_element_type=F32)


def _dot_nt(a, b):
    return lax.dot_general(a, b, (((1,), (1,)), ((), ())), preferred_element_type=F32)


def _dot_tn(a, b):
    return lax.dot_general(a, b, (((0,), (0,)), ((), ())), preferred_element_type=F32)


def _resident(block_shape, index_map):
    return pl.BlockSpec(block_shape, index_map, pipeline_mode=pl.Buffered(1))


def _prep_kernel(lbl_ref, lb_ref, cos_ref, sin_ref, cos_s_ref, sin_s_ref):
    x = lbl_ref[...]
    e = jnp.exp(x - jnp.max(x, axis=0, keepdims=True))
    sm = e / jnp.sum(e, axis=0, keepdims=True)
    depth = x.shape[0]
    acc = jnp.zeros((1, x.shape[1]), F32)
    rows = [acc]
    for l in range(1, depth):
        acc = acc + sm[l:l + 1, :]
        rows.append(acc)
    lb_ref[...] = jnp.concatenate(rows, axis=0)

    half = HEAD_DIM // 2

    def tables(shape, pos):
        lane = lax.broadcasted_iota(jnp.int32, shape, 1)
        j = (lane & (half - 1)).astype(F32)
        inv_freq = jnp.exp(j * (-math.log(ROPE_THETA) / half))
        ang = pos * inv_freq
        first = (lane & (HEAD_DIM - 1)) < half
        return jnp.cos(ang), jnp.where(first, -jnp.sin(ang), jnp.sin(ang))

    pos_p = lax.broadcasted_iota(jnp.int32, cos_ref.shape, 0).astype(F32)
    c, s = tables(cos_ref.shape, pos_p)
    cos_ref[...] = c
    sin_ref[...] = s
    c, s = tables(cos_s_ref.shape, jnp.full(cos_s_ref.shape, float(PAST_LEN), F32))
    cos_s_ref[...] = c
    sin_s_ref[...] = s


def _prep(lb_logits, seq, n_sample):
    depth = lb_logits.shape[0]
    out_shape = (jax.ShapeDtypeStruct((depth, R_KW), F32),
                 jax.ShapeDtypeStruct((seq, LANES), F32), jax.ShapeDtypeStruct((seq, LANES), F32),
                 jax.ShapeDtypeStruct((n_sample, LANES), F32), jax.ShapeDtypeStruct((n_sample, LANES), F32))
    return pl.pallas_call(_prep_kernel, out_shape=out_shape, name="prep")(lb_logits)


def _rope(x, cos, sin):
    w = x.shape[-1]
    reps = w // LANES
    if reps > 1:
        cos = jnp.tile(cos, (1, reps))
        sin = jnp.tile(sin, (1, reps))
    half = HEAD_DIM // 2
    lane = lax.broadcasted_iota(jnp.int32, x.shape, 1)
    first = (lane & (HEAD_DIM - 1)) < half
    swapped = jnp.where(first, pltpu.roll(x, w - half, 1), pltpu.roll(x, half, 1))
    return x * cos + swapped * sin


def _proj_in_kernel(l_ref, x_ref, g_ref, w_ref, cos_ref, sin_ref, q_ref, kv_ref, r_ref):
    h = _rms(x_ref[...], g_ref[...]).astype(BF16)
    cos = cos_ref[...]
    sin = sin_ref[...]
    q = _dot(h, w_ref[:, 0:ATTN_W])
    q_ref[...] = (_rope(q, cos, sin) * (HEAD_DIM ** -0.5)).astype(q_ref.dtype)
    k = _dot(h, w_ref[:, ATTN_W:ATTN_W + KV_W])
    kv_ref[:, 0:KV_W] = _rope(k, cos, sin)
    kv_ref[:, KV_W:2 * KV_W] = _dot(h, w_ref[:, ATTN_W + KV_W:ATTN_W + 2 * KV_W])
    r_ref[...] = _dot(h, w_ref[:, ATTN_W + 2 * KV_W:P_IN])


def _proj_in(layer, x, g, w_in, cos, sin, tm, q_dtype):
    t = x.shape[0]
    n_tab = cos.shape[0] // tm
    grid_spec = pltpu.PrefetchScalarGridSpec(
        num_scalar_prefetch=1, grid=(t // tm,),
        in_specs=[
            pl.BlockSpec((tm, D_MODEL), lambda i, l: (i, 0)),
            pl.BlockSpec((1, D_MODEL), lambda i, l: (0, 0)),
            _resident((None, D_MODEL, P_IN), lambda i, l: (l[0], 0, 0)),
            pl.BlockSpec((tm, LANES), lambda i, l: (i % n_tab, 0)),
            pl.BlockSpec((tm, LANES), lambda i, l: (i % n_tab, 0)),
        ],
        out_specs=[
            pl.BlockSpec((tm, ATTN_W), lambda i, l: (i, 0)),
            pl.BlockSpec((tm, 2 * KV_W), lambda i, l: (i, 0)),
            pl.BlockSpec((tm, R_IN_W), lambda i, l: (i, 0)),
        ])
    out_shape = (jax.ShapeDtypeStruct((t, ATTN_W), q_dtype),
                 jax.ShapeDtypeStruct((t, 2 * KV_W), F32),
                 jax.ShapeDtypeStruct((t, R_IN_W), F32))
    return pl.pallas_call(_proj_in_kernel, grid_spec=grid_spec, out_shape=out_shape,
                          compiler_params=_cparams(1), name="proj_in")(layer, x, g, w_in, cos, sin)


def _swa_kernel(q_ref, kv_ref, kvp_ref, sink_ref, o_ref, *, tiles_per_batch):
    i = pl.program_id(0)
    w = WINDOW
    nblk = q_ref.shape[0] // w
    first_key = jnp.where(i % tiles_per_batch == 0, w, 0)
    row = lax.broadcasted_iota(jnp.int32, (w, 2 * w), 0)
    col = lax.broadcasted_iota(jnp.int32, (w, 2 * w), 1)
    band = (col >= row) & (col <= row + w)
    low_o = lax.broadcasted_iota(jnp.int32, (w, LANES), 1) < HEAD_DIM
    low_v = lax.broadcasted_iota(jnp.int32, (2 * w, LANES), 1) < HEAD_DIM
    zeros_half = jnp.zeros((HEAD_DIM, 2 * w), BF16)
    for b in range(nblk):
        if b == 0:
            kv2 = jnp.concatenate([kvp_ref[...], kv_ref[0:w, :]], axis=0)
            valid = band & (col >= first_key)
        else:
            kv2 = kv_ref[(b - 1) * w:(b + 1) * w, :]
            valid = band
        kt = kv2[:, 0:KV_W].T.astype(BF16)
        v2 = kv2[:, KV_W:2 * KV_W]
        v_sw = pltpu.roll(v2, HEAD_DIM, 1)
        qb = q_ref[b * w:(b + 1) * w, :]
        k_pad, vcat = [], []
        for g in range(N_KV):
            kg = kt[g * HEAD_DIM:(g + 1) * HEAD_DIM, :]
            k_pad.append((jnp.concatenate([kg, zeros_half], axis=0), jnp.concatenate([zeros_half, kg], axis=0)))
            if g == 0:
                v_lo, v_hi = jnp.where(low_v, v2, 0.0), jnp.where(low_v, 0.0, v_sw)
            else:
                v_lo, v_hi = jnp.where(low_v, v_sw, 0.0), jnp.where(low_v, 0.0, v2)
            vcat.append(jnp.concatenate([v_lo, v_hi], axis=0).astype(BF16))
        scores = [jnp.where(valid, _dot(qb[:, (h // 2) * LANES:(h // 2 + 1) * LANES], k_pad[h // GQA_G][h % 2]),
                            -jnp.inf) for h in range(N_HEADS)]
        sinks = [sink_ref[0:1, h:h + 1] for h in range(N_HEADS)]
        maxes = [jnp.maximum(jnp.max(s, axis=-1, keepdims=True), sk) for s, sk in zip(scores, sinks)]
        probs = [jnp.exp(s - m) for s, m in zip(scores, maxes)]
        rden = [1.0 / (jnp.sum(p, axis=-1, keepdims=True) + jnp.exp(sk - m))
                for p, sk, m in zip(probs, sinks, maxes)]
        for pr in range(N_HEADS // 2):
            ps = slice(pr * LANES, (pr + 1) * LANES)
            pcat = jnp.concatenate([probs[2 * pr].astype(BF16), probs[2 * pr + 1].astype(BF16)], axis=1)
            o = _dot(pcat, vcat[(2 * pr) // GQA_G])
            o_ref[b * w:(b + 1) * w, ps] = (o * jnp.where(low_o, rden[2 * pr], rden[2 * pr + 1])).astype(o_ref.dtype)


def _swa(q, kv, sink, seq):
    t = q.shape[0]
    tq = min(SWA_TQ, seq)
    nblk = tq // WINDOW
    body = functools.partial(_swa_kernel, tiles_per_batch=seq // tq)
    return pl.pallas_call(
        body, grid=(t // tq,),
        in_specs=[
            pl.BlockSpec((tq, ATTN_W), lambda i: (i, 0)),
            pl.BlockSpec((tq, 2 * KV_W), lambda i: (i, 0)),
            pl.BlockSpec((WINDOW, 2 * KV_W), lambda i: (jnp.maximum(i * nblk - 1, 0), 0)),
            pl.BlockSpec((1, N_HEADS), lambda i: (0, 0)),
        ],
        out_specs=pl.BlockSpec((tq, ATTN_W), lambda i: (i, 0)),
        out_shape=jax.ShapeDtypeStruct(q.shape, BF16),
        compiler_params=_cparams(1), name="swa")(q, kv, kv, sink)


def _forget_and_key(z, lb):
    e = jnp.exp(-jnp.abs(z))
    logsig = jnp.minimum(z, 0.0) - jnp.log(1.0 + e)
    a = jnp.log(lb)
    c = jnp.log(1.0 - lb) + logsig
    logf = jnp.maximum(a, c) + jnp.log(1.0 + jnp.exp(-jnp.abs(a - c)))
    key = (1.0 - lb) * (jnp.where(z >= 0.0, e, 1.0) / (1.0 + e))
    return logf, key


def _cumsum_rows(g):
    n = g.shape[0]
    row = lax.broadcasted_iota(jnp.int32, g.shape, 0)
    b = g
    sh = 1
    while sh < n:
        if sh % SUBLANES == 0:
            shifted = jnp.concatenate([jnp.zeros((sh, g.shape[1]), F32), b[:n - sh]], axis=0)
        else:
            shifted = jnp.where(row >= sh, pltpu.roll(b, sh, 0), 0.0)
        b = b + shifted
        sh *= 2
    return b


def _level_ref(b, half):
    n_rows, width = b.shape
    n = 2 * half
    if n >= 2 * SUBLANES:
        pieces = [jnp.broadcast_to(b[i * n + half - 1:i * n + half, :], (n, width))
                  for i in range(n_rows // n)]
        return pieces[0] if len(pieces) == 1 else jnp.concatenate(pieces, axis=0)
    b3 = b.reshape(n_rows // SUBLANES, SUBLANES, width)
    sub = lax.broadcasted_iota(jnp.int32, b3.shape, 1)

    def bcast(r):
        return jnp.broadcast_to(b3[:, r:r + 1, :], b3.shape)

    if half == 4:
        ref = bcast(3)
    elif half == 2:
        ref = jnp.where(sub < 4, bcast(1), bcast(5))
    else:
        ref = jnp.where(sub < 2, bcast(0), jnp.where(sub < 4, bcast(2), jnp.where(sub < 6, bcast(4), bcast(6))))
    return ref.reshape(n_rows, width)


def _hgrn_kernel(qr_ref, fr_ref, ir_ref, gr_ref, lb_ref, gn_ref, o_ref, s_out_ref, st_ref):
    c = pl.program_id(1)
    last = pl.num_programs(1) - 1

    @pl.when(c == 0)
    def _():
        st_ref[...] = jnp.zeros_like(st_ref)

    rr = lax.broadcasted_iota(jnp.int32, (HC, HC), 0)
    cc = lax.broadcasted_iota(jnp.int32, (HC, HC), 1)
    xor = rr ^ cc
    causal = cc <= rr
    halves = [1 << i for i in range(HC.bit_length() - 1)]

    g, k = _forget_and_key(fr_ref[...], lb_ref[...])
    qr = qr_ref[...]
    q = qr * _sigmoid(qr)
    v16 = ir_ref[...].astype(BF16)
    b = _cumsum_rows(g)
    q16 = q.astype(BF16)
    k16 = k.astype(BF16)
    b_last = b[HC - 1:HC, :]
    q_in = (q * jnp.exp(b)).astype(BF16)
    k_out = (k * jnp.exp(b_last - b)).astype(BF16)
    decay = jnp.exp(b_last)
    gr = gr_ref[...]
    gate = gr * _sigmoid(gr)
    q_lv, k_lv = [], []
    for half in halves:
        e = jnp.exp(-jnp.abs(b - _level_ref(b, half))).astype(BF16)
        q_lv.append(q16 * e)
        k_lv.append(k16 * e)

    heads = [slice(h * R_DK, (h + 1) * R_DK) for h in range(R_HEADS)]
    diag = [_dot_nt(q16[:, sl], k16[:, sl]) for sl in heads]
    levels = [[_dot_nt(ql[:, sl], kl[:, sl]) for sl in heads] for ql, kl in zip(q_lv, k_lv)]
    att16 = []
    for h in range(R_HEADS):
        att = diag[h]
        for half, lv in zip(halves, levels):
            att = jnp.where(xor >= half, lv[h], att)
        att16.append(jnp.where(causal, att, 0.0).astype(BF16))
    st = [st_ref[h] for h in range(R_HEADS)]
    outs = [_dot(att16[h], v16[:, sl]) + _dot_nt(q_in[:, sl], st[h].astype(BF16))
            for h, sl in enumerate(heads)]
    st_new = [st[h] * decay[:, sl] + _dot_tn(v16[:, sl], k_out[:, sl]) for h, sl in enumerate(heads)]
    for h, sl in enumerate(heads):
        st_ref[h] = st_new[h]
        o = outs[h]
        on = o * lax.rsqrt(jnp.mean(o * o, axis=-1, keepdims=True) + EPS) * gn_ref[:, sl]
        o_ref[:, sl] = (on * gate[:, sl]).astype(o_ref.dtype)

    @pl.when(c == last)
    def _():
        for h in range(R_HEADS):
            s_out_ref[0, h] = st_new[h].T


def _hgrn(r, lb, gn, batch, seq):
    nc = seq // HC
    t = r.shape[0]

    def col(kk):
        return pl.BlockSpec((HC, R_KW), lambda b, c: (b * nc + c, kk))

    return pl.pallas_call(
        _hgrn_kernel, grid=(batch, nc),
        in_specs=[col(0), col(1), col(2), col(3),
                  pl.BlockSpec((1, R_KW), lambda b, c: (0, 0)),
                  pl.BlockSpec((1, R_VW), lambda b, c: (0, 0))],
        out_specs=[pl.BlockSpec((HC, R_VW), lambda b, c: (b * nc + c, 0)),
                   pl.BlockSpec((1, R_HEADS, R_DK, R_DV), lambda b, c: (b, 0, 0, 0))],
        out_shape=(jax.ShapeDtypeStruct((t, R_VW), BF16),
                   jax.ShapeDtypeStruct((batch, R_HEADS, R_DK, R_DV), F32)),
        scratch_shapes=[pltpu.VMEM((R_HEADS, R_DV, R_DK), F32)],
        compiler_params=_cparams(2), name="hgrn")(r, r, r, r, lb, gn)


def _mix_out_kernel(l_ref, a_ref, r_ref, w_ref, x_ref, g_ref, o_ref):
    m = (_dot(a_ref[...].astype(BF16), w_ref[0:ATTN_W, :])
         + _dot(r_ref[...].astype(BF16), w_ref[ATTN_W:ATTN_W + R_VW, :]))
    o_ref[...] = x_ref[...] + _rms(m, g_ref[...])


def _mix_out(layer, attn, rec, w_o, x, g, tm):
    t = x.shape[0]
    grid_spec = pltpu.PrefetchScalarGridSpec(
        num_scalar_prefetch=1, grid=(t // tm,),
        in_specs=[
            pl.BlockSpec((tm, ATTN_W), lambda i, l: (i, 0)),
            pl.BlockSpec((tm, R_VW), lambda i, l: (i, 0)),
            _resident((None, ATTN_W + R_VW, D_MODEL), lambda i, l: (l[0], 0, 0)),
            pl.BlockSpec((tm, D_MODEL), lambda i, l: (i, 0)),
            pl.BlockSpec((1, D_MODEL), lambda i, l: (0, 0)),
        ],
        out_specs=pl.BlockSpec((tm, D_MODEL), lambda i, l: (i, 0)))
    return pl.pallas_call(_mix_out_kernel, grid_spec=grid_spec,
                          out_shape=jax.ShapeDtypeStruct(x.shape, F32),
                          compiler_params=_cparams(1), name="mix_out")(layer, attn, rec, w_o, x, g)


def _memkv_kernel(m_ref, g_ref, wk_ref, wv_ref, k_ref, v_ref):
    h = _rms(m_ref[...], g_ref[...]).astype(BF16)
    k_ref[...] = _dot(h, wk_ref[...])
    v_ref[...] = _dot(h, wv_ref[...])


def _memkv(mem, g_mem, w_xk, w_xv):
    depth = w_xk.shape[0]
    rows = mem.shape[0]
    tm = min(TM, rows)
    out = jax.ShapeDtypeStruct((depth, rows, D_MODEL), F32)
    return pl.pallas_call(
        _memkv_kernel, grid=(depth, rows // tm),
        in_specs=[
            pl.BlockSpec((tm, D_MODEL), lambda l, i: (i, 0)),
            pl.BlockSpec((None, 1, D_MODEL), lambda l, i: (l, 0, 0)),
            pl.BlockSpec((None, D_MODEL, D_MODEL), lambda l, i: (l, 0, 0)),
            pl.BlockSpec((None, D_MODEL, D_MODEL), lambda l, i: (l, 0, 0)),
        ],
        out_specs=[pl.BlockSpec((None, tm, D_MODEL), lambda l, i: (l, i, 0)),
                   pl.BlockSpec((None, tm, D_MODEL), lambda l, i: (l, i, 0))],
        out_shape=(out, out), compiler_params=_cparams(2), name="memkv")(mem, g_mem, w_xk, w_xv)


def _xattn_kernel(l_ref, x_ref, gpre_ref, wq_ref, mk_ref, mv_ref, wo_ref, gpost_ref, o_ref):
    x = x_ref[...]
    h = _rms(x, gpre_ref[...]).astype(BF16)
    q = (_dot(h, wq_ref[...]) * (X_HEAD_DIM ** -0.5)).astype(BF16)
    outs = []
    for hd in range(X_HEADS):
        sl = slice(hd * X_HEAD_DIM, (hd + 1) * X_HEAD_DIM)
        s = _dot_nt(q[:, sl], mk_ref[:, sl].astype(BF16))
        p = jnp.exp(s - jnp.max(s, axis=-1, keepdims=True))
        den = jnp.sum(p, axis=-1, keepdims=True)
        outs.append((_dot(p.astype(BF16), mv_ref[:, sl].astype(BF16)) / den).astype(BF16))
    y = _dot(jnp.concatenate(outs, axis=1), wo_ref[...])
    o_ref[...] = x + _rms(y, gpost_ref[...])


def _xattn(layer, x, gpre, w_xq, mk, mv, w_xo, gpost, seq, tm):
    t = x.shape[0]
    tpb = seq // tm
    grid_spec = pltpu.PrefetchScalarGridSpec(
        num_scalar_prefetch=1, grid=(t // tm,),
        in_specs=[
            pl.BlockSpec((tm, D_MODEL), lambda i, l: (i, 0)),
            pl.BlockSpec((1, D_MODEL), lambda i, l: (0, 0)),
            _resident((None, D_MODEL, D_MODEL), lambda i, l: (l[0], 0, 0)),
            pl.BlockSpec((None, N_MEM, D_MODEL), lambda i, l: (l[0], i // tpb, 0)),
            pl.BlockSpec((None, N_MEM, D_MODEL), lambda i, l: (l[0], i // tpb, 0)),
            _resident((None, D_MODEL, D_MODEL), lambda i, l: (l[0], 0, 0)),
            pl.BlockSpec((1, D_MODEL), lambda i, l: (0, 0)),
        ],
        out_specs=pl.BlockSpec((tm, D_MODEL), lambda i, l: (i, 0)))
    return pl.pallas_call(_xattn_kernel, grid_spec=grid_spec,
                          out_shape=jax.ShapeDtypeStruct(x.shape, F32),
                          compiler_params=_cparams(1), name="xattn")(layer, x, gpre, w_xq, mk, mv, w_xo, gpost)


def _ffn_kernel(*refs, sample, tiles_per_batch):
    if sample:
        (l_ref, x_ref, gpre_ref, wup_ref, cw_ref, cb_ref, wdn_ref, gpost_ref, buf_ref,
         o_ref, tail_ref, act_ref) = refs
    else:
        (l_ref, x_ref, gpre_ref, wup_ref, cw_ref, cb_ref, wdn_ref, gpost_ref,
         o_ref, tail_ref, act_ref, carry_ref) = refs
        i = pl.program_id(0)

        @pl.when(i % tiles_per_batch == 0)
        def _():
            carry_ref[...] = jnp.zeros_like(carry_ref)

    x = x_ref[...]
    tm = x.shape[0]
    row = lax.broadcasted_iota(jnp.int32, (tm, FC), 0)
    h = _rms(x, gpre_ref[...]).astype(BF16)

    def conv(u, cs):
        w0 = cw_ref[0:1, cs]
        w1 = cw_ref[1:2, cs]
        w2 = cw_ref[2:3, cs]
        if sample:
            u2 = buf_ref[:, cs]
            u1 = buf_ref[:, slice(2 * D_FF + cs.start, 2 * D_FF + cs.stop)]
            tail_ref[:, cs] = u1
            tail_ref[:, slice(2 * D_FF + cs.start, 2 * D_FF + cs.stop)] = u
        else:
            p0 = carry_ref[0:1, cs]
            p1 = carry_ref[1:2, cs]
            u1 = jnp.where(row >= 1, pltpu.roll(u, 1, 0), p1)
            u2 = jnp.where(row >= 2, pltpu.roll(u, 2, 0), jnp.where(row == 1, p1, p0))
            carry_ref[:, cs] = u[tm - (CONV_W - 1):tm, :]
        return cb_ref[:, cs] + u2 * w0 + u1 * w1 + u * w2

    for j in range(D_FF // FC):
        ca_cols = slice(j * FC, (j + 1) * FC)
        cb_cols = slice(D_FF + j * FC, D_FF + (j + 1) * FC)
        ca = conv(_dot(h, wup_ref[:, ca_cols]), ca_cols)
        cb = conv(_dot(h, wup_ref[:, cb_cols]), cb_cols)
        act_ref[:, ca_cols] = (ca * _sigmoid(ca) * cb).astype(BF16)
    o_ref[...] = x + _rms(_dot(act_ref[...], wdn_ref[...]), gpost_ref[...])

    if not sample:
        @pl.when(i % tiles_per_batch == tiles_per_batch - 1)
        def _():
            tail_ref[0] = carry_ref[...]


def _ffn(layer, x, gpre, w_up, conv_w, conv_b, w_down, gpost, *, seq=None, tm=None, buf=None):
    t = x.shape[0]
    sample = buf is not None
    in_specs = [
        pl.BlockSpec((tm, D_MODEL), lambda i, l: (i, 0)),
        pl.BlockSpec((1, D_MODEL), lambda i, l: (0, 0)),
        _resident((None, D_MODEL, 2 * D_FF), lambda i, l: (l[0], 0, 0)),
        pl.BlockSpec((CONV_W, 2 * D_FF), lambda i, l: (0, 0)),
        pl.BlockSpec((1, 2 * D_FF), lambda i, l: (0, 0)),
        _resident((None, D_FF, D_MODEL), lambda i, l: (l[0], 0, 0)),
        pl.BlockSpec((1, D_MODEL), lambda i, l: (0, 0)),
    ]
    args = [layer, x, gpre, w_up, conv_w, conv_b, w_down, gpost]
    tail_w = (CONV_W - 1) * 2 * D_FF
    if sample:
        tiles_per_batch = 1
        in_specs.append(pl.BlockSpec((None, tm, tail_w), lambda i, l: (l[0], i, 0)))
        args.append(buf)
        tail_spec = pl.BlockSpec((tm, tail_w), lambda i, l: (i, 0))
        tail_shape = jax.ShapeDtypeStruct((t, tail_w), F32)
        scratch = [pltpu.VMEM((tm, D_FF), BF16)]
    else:
        tiles_per_batch = seq // tm
        batch = t // seq
        tail_spec = pl.BlockSpec((1, CONV_W - 1, 2 * D_FF), lambda i, l: (i // tiles_per_batch, 0, 0))
        tail_shape = jax.ShapeDtypeStruct((batch, CONV_W - 1, 2 * D_FF), F32)
        scratch = [pltpu.VMEM((tm, D_FF), BF16), pltpu.VMEM((CONV_W - 1, 2 * D_FF), F32)]
    grid_spec = pltpu.PrefetchScalarGridSpec(
        num_scalar_prefetch=1, grid=(t // tm,), in_specs=in_specs,
        out_specs=[pl.BlockSpec((tm, D_MODEL), lambda i, l: (i, 0)), tail_spec],
        scratch_shapes=scratch)
    body = functools.partial(_ffn_kernel, sample=sample, tiles_per_batch=tiles_per_batch)
    return pl.pallas_call(body, grid_spec=grid_spec,
                          out_shape=(jax.ShapeDtypeStruct(x.shape, F32), tail_shape),
                          compiler_params=_cparams(1),
                          name="ffn_sample" if sample else "ffn")(*args)


def _s_attn_kernel(l_ref, q_ref, kv_ref, ckt_ref, cvt_ref, sink_ref, o_ref):
    bt = q_ref.shape[0]
    for bi in range(bt):
        for g in range(N_KV):
            hs = slice(g * GQA_G, (g + 1) * GQA_G)
            ds = slice(g * HEAD_DIM, (g + 1) * HEAD_DIM)
            qg = q_ref[bi, hs, :]
            k_new = kv_ref[bi:bi + 1, ds]
            v_new = kv_ref[bi:bi + 1, KV_W + g * HEAD_DIM:KV_W + (g + 1) * HEAD_DIM]
            s = _dot(qg.astype(BF16), ckt_ref[bi, g].astype(BF16))
            s_new = jnp.sum(qg * k_new, axis=-1, keepdims=True)
            sk = sink_ref[hs, :]
            m = jnp.maximum(jnp.maximum(jnp.max(s, axis=-1, keepdims=True), s_new), sk)
            p = jnp.exp(s - m)
            p_new = jnp.exp(s_new - m)
            den = jnp.sum(p, axis=-1, keepdims=True) + p_new + jnp.exp(sk - m)
            o = _dot_nt(p.astype(BF16), cvt_ref[bi, g].astype(BF16)) + p_new * v_new
            o_ref[bi, hs, :] = o / den


def _s_attn(layer, q, kv, cache_kt, cache_vt, sink_col):
    n = q.shape[0]
    w = cache_kt.shape[-1]
    grid_spec = pltpu.PrefetchScalarGridSpec(
        num_scalar_prefetch=1, grid=(n // S_BT,),
        in_specs=[
            pl.BlockSpec((S_BT, N_HEADS, HEAD_DIM), lambda i, l: (i, 0, 0)),
            pl.BlockSpec((S_BT, 2 * KV_W), lambda i, l: (i, 0)),
            pl.BlockSpec((None, S_BT, N_KV, HEAD_DIM, w), lambda i, l: (l[0], i, 0, 0, 0)),
            pl.BlockSpec((None, S_BT, N_KV, HEAD_DIM, w), lambda i, l: (l[0], i, 0, 0, 0)),
            pl.BlockSpec((N_HEADS, 1), lambda i, l: (0, 0)),
        ],
        out_specs=pl.BlockSpec((S_BT, N_HEADS, HEAD_DIM), lambda i, l: (i, 0, 0)))
    return pl.pallas_call(_s_attn_kernel, grid_spec=grid_spec,
                          out_shape=jax.ShapeDtypeStruct((n, N_HEADS, HEAD_DIM), F32),
                          compiler_params=_cparams(1), name="s_attn")(
                              layer, q.reshape(n, N_HEADS, HEAD_DIM), kv, cache_kt, cache_vt, sink_col
                          ).reshape(n, ATTN_W)


def _s_hgrn_kernel(l_ref, r_ref, rt_ref, st_ref, lbt_ref, gn_ref, o_ref, so_ref):
    bt = r_ref.shape[0]
    for h in range(R_HEADS):
        sl = slice(h * R_DK, (h + 1) * R_DK)
        zt = rt_ref[R_KW + h * R_DK:R_KW + (h + 1) * R_DK, :]
        lbc = lbt_ref[sl, :]
        logf, kt = _forget_and_key(zt, lbc)
        f = jnp.exp(logf)
        qrt = rt_ref[sl, :]
        qt = qrt * _sigmoid(qrt)
        for bi in range(bt):
            s = st_ref[bi, h]
            v = r_ref[bi:bi + 1, 2 * R_KW + h * R_DV:2 * R_KW + (h + 1) * R_DV]
            s_new = f[:, bi:bi + 1] * s + kt[:, bi:bi + 1] * v
            so_ref[bi, h] = s_new
            o = jnp.sum(qt[:, bi:bi + 1] * s_new, axis=0, keepdims=True)
            on = o * lax.rsqrt(jnp.mean(o * o, axis=-1, keepdims=True) + EPS) * gn_ref[:, sl]
            gr = r_ref[bi:bi + 1, 2 * R_KW + R_VW + h * R_DV:2 * R_KW + R_VW + (h + 1) * R_DV]
            o_ref[bi:bi + 1, sl] = (on * (gr * _sigmoid(gr))).astype(o_ref.dtype)


def _s_hgrn(layer, r, state, lb_col, gn):
    n = r.shape[0]
    rt = r.reshape(n // S_BT, S_BT, R_IN_W).transpose(0, 2, 1)
    grid_spec = pltpu.PrefetchScalarGridSpec(
        num_scalar_prefetch=1, grid=(n // S_BT,),
        in_specs=[
            pl.BlockSpec((S_BT, R_IN_W), lambda i, l: (i, 0)),
            pl.BlockSpec((None, R_IN_W, S_BT), lambda i, l: (i, 0, 0)),
            pl.BlockSpec((None, S_BT, R_HEADS, R_DK, R_DV), lambda i, l: (l[0], i, 0, 0, 0)),
            pl.BlockSpec((R_KW, 1), lambda i, l: (0, 0)),
            pl.BlockSpec((1, R_VW), lambda i, l: (0, 0)),
        ],
        out_specs=[pl.BlockSpec((S_BT, R_VW), lambda i, l: (i, 0)),
                   pl.BlockSpec((S_BT, R_HEADS, R_DK, R_DV), lambda i, l: (i, 0, 0, 0))])
    return pl.pallas_call(_s_hgrn_kernel, grid_spec=grid_spec,
                          out_shape=(jax.ShapeDtypeStruct((n, R_VW), F32),
                                     jax.ShapeDtypeStruct((n, R_HEADS, R_DK, R_DV), F32)),
                          compiler_params=_cparams(1), name="s_hgrn")(layer, r, rt, state, lb_col, gn)


def _s_xattn_kernel(l_ref, x_ref, gpre_ref, wq_ref, ck_ref, cv_ref, wo_ref, gpost_ref, o_ref):
    bt = ck_ref.shape[0]
    x = x_ref[...]
    h = _rms(x, gpre_ref[...]).astype(BF16)
    q = _dot(h, wq_ref[...]) * (X_HEAD_DIM ** -0.5)
    rows = []
    for bi in range(bt):
        qb = jnp.concatenate([q[bi:bi + 1, hd * X_HEAD_DIM:(hd + 1) * X_HEAD_DIM]
                              for hd in range(X_HEADS)], axis=0)
        s = jnp.sum(ck_ref[bi] * qb[None], axis=-1, keepdims=True)
        p = jnp.exp(s - jnp.max(s, axis=0, keepdims=True))
        den = jnp.sum(p, axis=0)
        o = jnp.sum(p * cv_ref[bi], axis=0) / den
        rows.append(jnp.concatenate([o[hd:hd + 1, :] for hd in range(X_HEADS)], axis=1))
    y = _dot(jnp.concatenate(rows, axis=0).astype(BF16), wo_ref[...])
    o_ref[...] = x + _rms(y, gpost_ref[...])


def _s_xattn(layer, x, gpre, w_xq, cache_k, cache_v, w_xo, gpost):
    n = x.shape[0]
    grid_spec = pltpu.PrefetchScalarGridSpec(
        num_scalar_prefetch=1, grid=(n // X_BT,),
        in_specs=[
            pl.BlockSpec((X_BT, D_MODEL), lambda i, l: (i, 0)),
            pl.BlockSpec((1, D_MODEL), lambda i, l: (0, 0)),
            _resident((None, D_MODEL, D_MODEL), lambda i, l: (l[0], 0, 0)),
            pl.BlockSpec((None, X_BT, N_MEM, X_HEADS, X_HEAD_DIM), lambda i, l: (l[0], i, 0, 0, 0)),
            pl.BlockSpec((None, X_BT, N_MEM, X_HEADS, X_HEAD_DIM), lambda i, l: (l[0], i, 0, 0, 0)),
            _resident((None, D_MODEL, D_MODEL), lambda i, l: (l[0], 0, 0)),
            pl.BlockSpec((1, D_MODEL), lambda i, l: (0, 0)),
        ],
        out_specs=pl.BlockSpec((X_BT, D_MODEL), lambda i, l: (i, 0)))
    return pl.pallas_call(_s_xattn_kernel, grid_spec=grid_spec,
                          out_shape=jax.ShapeDtypeStruct(x.shape, F32),
                          compiler_params=_cparams(1), name="s_xattn")(
                              layer, x, gpre, w_xq, cache_k, cache_v, w_xo, gpost)


def kernel(x_prompt, x_sample, cache_win_k, cache_win_v, cache_mem_k, cache_mem_v, state_hgrn, cache_ffn_conv, mem_prompt, w_in, w_o, attn_sinks, lb_logits, hgrn_norm, w_xq, w_xk, w_xv, w_xo, w_up, conv_w, conv_b, w_down, g_pre_mix, g_post_mix, g_pre_x, g_post_x, g_mem, g_pre_ffn, g_post_ffn):
    batch, seq, _ = x_prompt.shape
    n_s, dec_seq, _ = x_sample.shape
    depth = w_in.shape[0]
    assert dec_seq == 1 and seq % WINDOW == 0 and seq % HC == 0
    assert cache_win_k.shape[2] == WINDOW and n_s % S_BT == 0 and n_s % X_BT == 0
    tm = min(TM, seq)
    assert seq % tm == 0

    w_in_b, w_o_b, w_xq_b, w_xk_b, w_xv_b, w_xo_b, w_up_b, w_down_b = (
        w.astype(BF16) for w in (w_in, w_o, w_xq, w_xk, w_xv, w_xo, w_up, w_down))
    lb, cos_p, sin_p, cos_s, sin_s = _prep(lb_logits, seq, n_s)
    mk, mv = _memkv(mem_prompt.reshape(batch * N_MEM, D_MODEL), g_mem.reshape(depth, 1, D_MODEL),
                    w_xk_b, w_xv_b)

    cwk = jnp.transpose(cache_win_k, (0, 1, 3, 4, 2))
    cwv = jnp.transpose(cache_win_v, (0, 1, 3, 4, 2))
    cmk, cmv = cache_mem_k, cache_mem_v
    cfc = cache_ffn_conv.reshape(depth, n_s, (CONV_W - 1) * 2 * D_FF)

    def layer_fn(carry, l):
        xp, xs = carry
        layer = jnp.reshape(l, (1,)).astype(jnp.int32)

        def vec(a):
            return lax.dynamic_index_in_dim(a, l, 0, keepdims=True)

        sink, lb_l, gn_l = vec(attn_sinks), vec(lb), vec(hgrn_norm)
        cw_l, cb_l = lax.dynamic_index_in_dim(conv_w, l, 0, keepdims=False), vec(conv_b)
        gpm, gom, gpx, gox, gpf, gof = (vec(g) for g in (g_pre_mix, g_post_mix, g_pre_x, g_post_x,
                                                        g_pre_ffn, g_post_ffn))

        q_p, kv_p, r_p = _proj_in(layer, xp, gpm, w_in_b, cos_p, sin_p, tm, BF16)
        attn_p = _swa(q_p, kv_p, sink, seq)
        rec_p, st_p = _hgrn(r_p, lb_l, gn_l, batch, seq)
        xp = _mix_out(layer, attn_p, rec_p, w_o_b, xp, gom, tm)

        q_s, kv_s, r_s = _proj_in(layer, xs, gpm, w_in_b, cos_s, sin_s, n_s, F32)
        attn_s = _s_attn(layer, q_s, kv_s, cwk, cwv, sink.reshape(N_HEADS, 1))
        rec_s, st_s = _s_hgrn(layer, r_s, state_hgrn, lb_l.reshape(R_KW, 1), gn_l)
        xs = _mix_out(layer, attn_s, rec_s, w_o_b, xs, gom, n_s)

        xp = _xattn(layer, xp, gpx, w_xq_b, mk, mv, w_xo_b, gox, seq, tm)
        xs = _s_xattn(layer, xs, gpx, w_xq_b, cmk, cmv, w_xo_b, gox)

        xp, tail_p = _ffn(layer, xp, gpf, w_up_b, cw_l, cb_l, w_down_b, gof, seq=seq, tm=tm)
        xs, tail_s = _ffn(layer, xs, gpf, w_up_b, cw_l, cb_l, w_down_b, gof, tm=n_s, buf=cfc)

        kv_keep = kv_p.reshape(batch, seq, 2 * KV_W)[:, seq - WINDOW:, :]
        ys = (kv_keep[..., :KV_W], kv_keep[..., KV_W:], kv_s[:, :KV_W], kv_s[:, KV_W:],
              st_p, st_s, tail_p, tail_s)
        return (xp, xs), ys

    (xp, xs), ys = lax.scan(layer_fn, (x_prompt.reshape(batch * seq, D_MODEL), x_sample.reshape(n_s, D_MODEL)),
                            jnp.arange(depth, dtype=jnp.int32))
    wk_p, wv_p, wk_s, wv_s, st_p, st_s, tail_p, tail_s = ys
    return (xp.reshape(batch, seq, D_MODEL), xs.reshape(n_s, 1, D_MODEL),
            wk_p.reshape(depth, batch, WINDOW, N_KV, HEAD_DIM), wv_p.reshape(depth, batch, WINDOW, N_KV, HEAD_DIM),
            wk_s.reshape(depth, n_s, 1, N_KV, HEAD_DIM), wv_s.reshape(depth, n_s, 1, N_KV, HEAD_DIM),
            mk.reshape(depth, batch, N_MEM, X_HEADS, X_HEAD_DIM), mv.reshape(depth, batch, N_MEM, X_HEADS, X_HEAD_DIM),
            st_p, st_s,
            tail_p, tail_s.reshape(depth, n_s, CONV_W - 1, 2 * D_FF))
```

```python
import functools
import math

import jax
import jax.numpy as jnp
from jax import lax
from jax.experimental import pallas as pl
from jax.experimental.pallas import tpu as pltpu

F32 = jnp.float32
BF16 = jnp.bfloat16

D_MODEL = 1024
HEAD_DIM = 64
N_HEADS = 8
N_KV = 2
GQA_G = N_HEADS // N_KV
WINDOW = 128
ROPE_THETA = 10000.0
R_HEADS = 4
R_DK = 128
R_DV = 128
ATTN_W = N_HEADS * HEAD_DIM
KV_W = N_KV * HEAD_DIM
R_KW = R_HEADS * R_DK
R_VW = R_HEADS * R_DV
R_IN_W = 2 * R_KW + 2 * R_VW
P_IN = ATTN_W + 2 * KV_W + R_IN_W
N_MEM = 256
X_HEADS = 4
X_HEAD_DIM = D_MODEL // X_HEADS
D_FF = 2816
CONV_W = 3
EPS = 1e-6
PAST_LEN = 8192

LANES = 128
SUBLANES = 8
VMEM_LIMIT = 56 * 1024 * 1024

TM = 512
SWA_TQ = 512
HC = 128
HG_ROWS = 512
LOG2_E = 1.4426950408889634
FC = 256
S_BT = 8
X_BT = 8


def _cparams(n_axes):
    return pltpu.CompilerParams(dimension_semantics=("arbitrary",) * n_axes,
                                vmem_limit_bytes=VMEM_LIMIT)


def _rms(x, g):
    return x * lax.rsqrt(jnp.mean(x * x, axis=-1, keepdims=True) + EPS) * g


def _sigmoid(x):
    return 1.0 / (1.0 + jnp.exp(-x))


def _dot(a, b):
    return jnp.dot(a, b, preferred_element_type=F32)


def _dot_nt(a, b):
    return lax.dot_general(a, b, (((1,), (1,)), ((), ())), preferred_element_type=F32)


def _dot_tn(a, b):
    return lax.dot_general(a, b, (((0,), (0,)), ((), ())), preferred_element_type=F32)


def _resident(block_shape, index_map):
    return pl.BlockSpec(block_shape, index_map, pipeline_mode=pl.Buffered(1))


def _prep_kernel(lbl_ref, lb_ref, cos_ref, sin_ref, cos_s_ref, sin_s_ref):
    x = lbl_ref[...]
    e = jnp.exp(x - jnp.max(x, axis=0, keepdims=True))
    sm = e / jnp.sum(e, axis=0, keepdims=True)
    depth = x.shape[0]
    acc = jnp.zeros((1, x.shape[1]), F32)
    rows = [acc]
    for l in range(1, depth):
        acc = acc + sm[l:l + 1, :]
        rows.append(acc)
    lb_ref[...] = jnp.concatenate(rows, axis=0)

    half = HEAD_DIM // 2

    def tables(shape, pos):
        lane = lax.broadcasted_iota(jnp.int32, shape, 1)
        j = (lane & (half - 1)).astype(F32)
        inv_freq = jnp.exp(j * (-math.log(ROPE_THETA) / half))
        ang = pos * inv_freq
        first = (lane & (HEAD_DIM - 1)) < half
        return jnp.cos(ang), jnp.where(first, -jnp.sin(ang), jnp.sin(ang))

    pos_p = lax.broadcasted_iota(jnp.int32, cos_ref.shape, 0).astype(F32)
    c, s = tables(cos_ref.shape, pos_p)
    cos_ref[...] = c
    sin_ref[...] = s
    c, s = tables(cos_s_ref.shape, jnp.full(cos_s_ref.shape, float(PAST_LEN), F32))
    cos_s_ref[...] = c
    sin_s_ref[...] = s


def _prep(lb_logits, seq, n_sample):
    depth = lb_logits.shape[0]
    out_shape = (jax.ShapeDtypeStruct((depth, R_KW), F32),
                 jax.ShapeDtypeStruct((seq, LANES), F32), jax.ShapeDtypeStruct((seq, LANES), F32),
                 jax.ShapeDtypeStruct((n_sample, LANES), F32), jax.ShapeDtypeStruct((n_sample, LANES), F32))
    return pl.pallas_call(_prep_kernel, out_shape=out_shape, name="prep")(lb_logits)


def _rope(x, cos, sin):
    w = x.shape[-1]
    reps = w // LANES
    if reps > 1:
        cos = jnp.tile(cos, (1, reps))
        sin = jnp.tile(sin, (1, reps))
    half = HEAD_DIM // 2
    lane = lax.broadcasted_iota(jnp.int32, x.shape, 1)
    first = (lane & (HEAD_DIM - 1)) < half
    swapped = jnp.where(first, pltpu.roll(x, w - half, 1), pltpu.roll(x, half, 1))
    return x * cos + swapped * sin


def _proj_in_kernel(l_ref, x_ref, g_ref, w_ref, cos_ref, sin_ref, q_ref, kv_ref, r_ref):
    h = _rms(x_ref[...], g_ref[...]).astype(BF16)
    cos = cos_ref[...]
    sin = sin_ref[...]
    q = _dot(h, w_ref[:, 0:ATTN_W])
    q_ref[...] = (_rope(q, cos, sin) * (HEAD_DIM ** -0.5)).astype(q_ref.dtype)
    k = _dot(h, w_ref[:, ATTN_W:ATTN_W + KV_W])
    kv_ref[:, 0:KV_W] = _rope(k, cos, sin)
    kv_ref[:, KV_W:2 * KV_W] = _dot(h, w_ref[:, ATTN_W + KV_W:ATTN_W + 2 * KV_W])
    r_ref[...] = _dot(h, w_ref[:, ATTN_W + 2 * KV_W:P_IN])


def _proj_in(layer, x, g, w_in, cos, sin, tm, q_dtype):
    t = x.shape[0]
    n_tab = cos.shape[0] // tm
    grid_spec = pltpu.PrefetchScalarGridSpec(
        num_scalar_prefetch=1, grid=(t // tm,),
        in_specs=[
            pl.BlockSpec((tm, D_MODEL), lambda i, l: (i, 0)),
            pl.BlockSpec((1, D_MODEL), lambda i, l: (0, 0)),
            _resident((None, D_MODEL, P_IN), lambda i, l: (l[0], 0, 0)),
            pl.BlockSpec((tm, LANES), lambda i, l: (i % n_tab, 0)),
            pl.BlockSpec((tm, LANES), lambda i, l: (i % n_tab, 0)),
        ],
        out_specs=[
            pl.BlockSpec((tm, ATTN_W), lambda i, l: (i, 0)),
            pl.BlockSpec((tm, 2 * KV_W), lambda i, l: (i, 0)),
            pl.BlockSpec((tm, R_IN_W), lambda i, l: (i, 0)),
        ])
    out_shape = (jax.ShapeDtypeStruct((t, ATTN_W), q_dtype),
                 jax.ShapeDtypeStruct((t, 2 * KV_W), F32),
                 jax.ShapeDtypeStruct((t, R_IN_W), F32))
    return pl.pallas_call(_proj_in_kernel, grid_spec=grid_spec, out_shape=out_shape,
                          compiler_params=_cparams(1), name="proj_in")(layer, x, g, w_in, cos, sin)


def _swa_kernel(q_ref, kv_ref, kvp_ref, sink_ref, o_ref, *, tiles_per_batch):
    i = pl.program_id(0)
    w = WINDOW
    nblk = q_ref.shape[0] // w
    first_key = jnp.where(i % tiles_per_batch == 0, w, 0)
    row = lax.broadcasted_iota(jnp.int32, (w, 2 * w), 0)
    col = lax.broadcasted_iota(jnp.int32, (w, 2 * w), 1)
    band = (col >= row) & (col <= row + w)
    low_o = lax.broadcasted_iota(jnp.int32, (w, LANES), 1) < HEAD_DIM
    low_v = lax.broadcasted_iota(jnp.int32, (2 * w, LANES), 1) < HEAD_DIM
    zeros_half = jnp.zeros((HEAD_DIM, 2 * w), BF16)
    for b in range(nblk):
        if b == 0:
            kv2 = jnp.concatenate([kvp_ref[...], kv_ref[0:w, :]], axis=0)
            valid = band & (col >= first_key)
        else:
            kv2 = kv_ref[(b - 1) * w:(b + 1) * w, :]
            valid = band
        kt = kv2[:, 0:KV_W].T.astype(BF16)
        v2 = kv2[:, KV_W:2 * KV_W]
        v_sw = pltpu.roll(v2, HEAD_DIM, 1)
        qb = q_ref[b * w:(b + 1) * w, :]
        k_pad, vcat = [], []
        for g in range(N_KV):
            kg = kt[g * HEAD_DIM:(g + 1) * HEAD_DIM, :]
            k_pad.append((jnp.concatenate([kg, zeros_half], axis=0), jnp.concatenate([zeros_half, kg], axis=0)))
            if g == 0:
                v_lo, v_hi = jnp.where(low_v, v2, 0.0), jnp.where(low_v, 0.0, v_sw)
            else:
                v_lo, v_hi = jnp.where(low_v, v_sw, 0.0), jnp.where(low_v, 0.0, v2)
            vcat.append(jnp.concatenate([v_lo, v_hi], axis=0).astype(BF16))
        scores = [jnp.where(valid, _dot(qb[:, (h // 2) * LANES:(h // 2 + 1) * LANES], k_pad[h // GQA_G][h % 2]),
                            -jnp.inf) for h in range(N_HEADS)]
        sinks = [sink_ref[0:1, h:h + 1] for h in range(N_HEADS)]
        maxes = [jnp.maximum(jnp.max(s, axis=-1, keepdims=True), sk) for s, sk in zip(scores, sinks)]
        probs = [jnp.exp(s - m) for s, m in zip(scores, maxes)]
        rden = [1.0 / (jnp.sum(p, axis=-1, keepdims=True) + jnp.exp(sk - m))
                for p, sk, m in zip(probs, sinks, maxes)]
        for pr in range(N_HEADS // 2):
            ps = slice(pr * LANES, (pr + 1) * LANES)
            pcat = jnp.concatenate([probs[2 * pr].astype(BF16), probs[2 * pr + 1].astype(BF16)], axis=1)
            o = _dot(pcat, vcat[(2 * pr) // GQA_G])
            o_ref[b * w:(b + 1) * w, ps] = (o * jnp.where(low_o, rden[2 * pr], rden[2 * pr + 1])).astype(o_ref.dtype)


def _swa(q, kv, sink, seq):
    t = q.shape[0]
    tq = min(SWA_TQ, seq)
    nblk = tq // WINDOW
    body = functools.partial(_swa_kernel, tiles_per_batch=seq // tq)
    return pl.pallas_call(
        body, grid=(t // tq,),
        in_specs=[
            pl.BlockSpec((tq, ATTN_W), lambda i: (i, 0)),
            pl.BlockSpec((tq, 2 * KV_W), lambda i: (i, 0)),
            pl.BlockSpec((WINDOW, 2 * KV_W), lambda i: (jnp.maximum(i * nblk - 1, 0), 0)),
            pl.BlockSpec((1, N_HEADS), lambda i: (0, 0)),
        ],
        out_specs=pl.BlockSpec((tq, ATTN_W), lambda i: (i, 0)),
        out_shape=jax.ShapeDtypeStruct(q.shape, BF16),
        compiler_params=_cparams(1), name="swa")(q, kv, kv, sink)


def _forget_and_key(z, lb):
    e = jnp.exp(-jnp.abs(z))
    logsig = jnp.minimum(z, 0.0) - jnp.log(1.0 + e)
    a = jnp.log(lb)
    c = jnp.log(1.0 - lb) + logsig
    logf = jnp.maximum(a, c) + jnp.log(1.0 + jnp.exp(-jnp.abs(a - c)))
    key = (1.0 - lb) * (jnp.where(z >= 0.0, e, 1.0) / (1.0 + e))
    return logf, key


def _cumsum_rows(g):
    n = g.shape[0]
    row = lax.broadcasted_iota(jnp.int32, g.shape, 0)
    b = g
    sh = 1
    while sh < n:
        if sh % SUBLANES == 0:
            shifted = jnp.concatenate([jnp.zeros((sh, g.shape[1]), F32), b[:n - sh]], axis=0)
        else:
            shifted = jnp.where(row >= sh, pltpu.roll(b, sh, 0), 0.0)
        b = b + shifted
        sh *= 2
    return b


def _level_ref(b, half):
    n_rows, width = b.shape
    n = 2 * half
    if n >= 2 * SUBLANES:
        pieces = [jnp.broadcast_to(b[i * n + half - 1:i * n + half, :], (n, width))
                  for i in range(n_rows // n)]
        return pieces[0] if len(pieces) == 1 else jnp.concatenate(pieces, axis=0)
    b3 = b.reshape(n_rows // SUBLANES, SUBLANES, width)
    sub = lax.broadcasted_iota(jnp.int32, b3.shape, 1)

    def bcast(r):
        return jnp.broadcast_to(b3[:, r:r + 1, :], b3.shape)

    if half == 4:
        ref = bcast(3)
    elif half == 2:
        ref = jnp.where(sub < 4, bcast(1), bcast(5))
    else:
        ref = jnp.where(sub < 2, bcast(0), jnp.where(sub < 4, bcast(2), jnp.where(sub < 6, bcast(4), bcast(6))))
    return ref.reshape(n_rows, width)


def _hgrn_kernel(qr_ref, fr_ref, ir_ref, gr_ref, lb_ref, gn_ref, o_ref, s_out_ref, st_ref):
    c = pl.program_id(1)
    last = pl.num_programs(1) - 1

    @pl.when(c == 0)
    def _():
        st_ref[...] = jnp.zeros_like(st_ref)

    rr = lax.broadcasted_iota(jnp.int32, (HC, HC), 0)
    cc = lax.broadcasted_iota(jnp.int32, (HC, HC), 1)
    xor = rr ^ cc
    causal = cc <= rr
    halves = [1 << i for i in range(HC.bit_length() - 1)]
    heads = [slice(h * R_DK, (h + 1) * R_DK) for h in range(R_HEADS)]
    st = [st_ref[h] for h in range(R_HEADS)]

    for ci in range(qr_ref.shape[0] // HC):
        rows = slice(ci * HC, (ci + 1) * HC)
        g, k = _forget_and_key(fr_ref[rows, :], lb_ref[...])
        qr = qr_ref[rows, :]
        q = qr * _sigmoid(qr)
        v16 = ir_ref[rows, :].astype(BF16)
        b = _cumsum_rows(g * LOG2_E)
        q16 = q.astype(BF16)
        k16 = k.astype(BF16)
        b_last = b[HC - 1:HC, :]
        q_in = (q * jnp.exp2(b)).astype(BF16)
        k_out = (k * jnp.exp2(b_last - b)).astype(BF16)
        decay = jnp.exp2(b_last)
        gr = gr_ref[rows, :]
        gate = gr * _sigmoid(gr)
        q_lv, k_lv = [], []
        for half in halves:
            e = jnp.exp2(-jnp.abs(b - _level_ref(b, half))).astype(BF16)
            q_lv.append(q16 * e)
            k_lv.append(k16 * e)

        diag = [_dot_nt(q16[:, sl], k16[:, sl]) for sl in heads]
        levels = [[_dot_nt(ql[:, sl], kl[:, sl]) for sl in heads] for ql, kl in zip(q_lv, k_lv)]
        att16 = []
        for h in range(R_HEADS):
            att = diag[h]
            for half, lv in zip(halves, levels):
                att = jnp.where(xor >= half, lv[h], att)
            att16.append(jnp.where(causal, att, 0.0).astype(BF16))
        outs = [_dot(att16[h], v16[:, sl]) + _dot_nt(q_in[:, sl], st[h].astype(BF16))
                for h, sl in enumerate(heads)]
        st = [st[h] * decay[:, sl] + _dot_tn(v16[:, sl], k_out[:, sl]) for h, sl in enumerate(heads)]
        for h, sl in enumerate(heads):
            o = outs[h]
            on = o * lax.rsqrt(jnp.mean(o * o, axis=-1, keepdims=True) + EPS) * gn_ref[:, sl]
            o_ref[rows, sl] = (on * gate[:, sl]).astype(o_ref.dtype)

    for h in range(R_HEADS):
        st_ref[h] = st[h]

    @pl.when(c == last)
    def _():
        for h in range(R_HEADS):
            s_out_ref[0, h] = st[h].T


def _hgrn(r, lb, gn, batch, seq):
    rows = min(HG_ROWS, seq)
    nc = seq // rows
    t = r.shape[0]

    def col(kk):
        return pl.BlockSpec((rows, R_KW), lambda b, c: (b * nc + c, kk))

    return pl.pallas_call(
        _hgrn_kernel, grid=(batch, nc),
        in_specs=[col(0), col(1), col(2), col(3),
                  pl.BlockSpec((1, R_KW), lambda b, c: (0, 0)),
                  pl.BlockSpec((1, R_VW), lambda b, c: (0, 0))],
        out_specs=[pl.BlockSpec((rows, R_VW), lambda b, c: (b * nc + c, 0)),
                   pl.BlockSpec((1, R_HEADS, R_DK, R_DV), lambda b, c: (b, 0, 0, 0))],
        out_shape=(jax.ShapeDtypeStruct((t, R_VW), BF16),
                   jax.ShapeDtypeStruct((batch, R_HEADS, R_DK, R_DV), F32)),
        scratch_shapes=[pltpu.VMEM((R_HEADS, R_DV, R_DK), F32)],
        compiler_params=_cparams(2), name="hgrn")(r, r, r, r, lb, gn)


def _mix_out_kernel(l_ref, a_ref, r_ref, w_ref, x_ref, g_ref, o_ref):
    m = (_dot(a_ref[...].astype(BF16), w_ref[0:ATTN_W, :])
         + _dot(r_ref[...].astype(BF16), w_ref[ATTN_W:ATTN_W + R_VW, :]))
    o_ref[...] = x_ref[...] + _rms(m, g_ref[...])


def _mix_out(layer, attn, rec, w_o, x, g, tm):
    t = x.shape[0]
    grid_spec = pltpu.PrefetchScalarGridSpec(
        num_scalar_prefetch=1, grid=(t // tm,),
        in_specs=[
            pl.BlockSpec((tm, ATTN_W), lambda i, l: (i, 0)),
            pl.BlockSpec((tm, R_VW), lambda i, l: (i, 0)),
            _resident((None, ATTN_W + R_VW, D_MODEL), lambda i, l: (l[0], 0, 0)),
            pl.BlockSpec((tm, D_MODEL), lambda i, l: (i, 0)),
            pl.BlockSpec((1, D_MODEL), lambda i, l: (0, 0)),
        ],
        out_specs=pl.BlockSpec((tm, D_MODEL), lambda i, l: (i, 0)))
    return pl.pallas_call(_mix_out_kernel, grid_spec=grid_spec,
                          out_shape=jax.ShapeDtypeStruct(x.shape, F32),
                          compiler_params=_cparams(1), name="mix_out")(layer, attn, rec, w_o, x, g)


def _memkv_kernel(m_ref, g_ref, wk_ref, wv_ref, k_ref, v_ref):
    h = _rms(m_ref[...], g_ref[...]).astype(BF16)
    k_ref[...] = _dot(h, wk_ref[...])
    v_ref[...] = _dot(h, wv_ref[...])


def _memkv(mem, g_mem, w_xk, w_xv):
    depth = w_xk.shape[0]
    rows = mem.shape[0]
    tm = min(TM, rows)
    out = jax.ShapeDtypeStruct((depth, rows, D_MODEL), F32)
    return pl.pallas_call(
        _memkv_kernel, grid=(depth, rows // tm),
        in_specs=[
            pl.BlockSpec((tm, D_MODEL), lambda l, i: (i, 0)),
            pl.BlockSpec((None, 1, D_MODEL), lambda l, i: (l, 0, 0)),
            pl.BlockSpec((None, D_MODEL, D_MODEL), lambda l, i: (l, 0, 0)),
            pl.BlockSpec((None, D_MODEL, D_MODEL), lambda l, i: (l, 0, 0)),
        ],
        out_specs=[pl.BlockSpec((None, tm, D_MODEL), lambda l, i: (l, i, 0)),
                   pl.BlockSpec((None, tm, D_MODEL), lambda l, i: (l, i, 0))],
        out_shape=(out, out), compiler_params=_cparams(2), name="memkv")(mem, g_mem, w_xk, w_xv)


def _mix_xattn_kernel(l_ref, a_ref, r_ref, wmix_ref, x_ref, gmix_ref,
                      gpre_ref, wq_ref, mk_ref, mv_ref, wo_ref, gpost_ref, o_ref):
    m = (_dot(a_ref[...], wmix_ref[0:ATTN_W, :]) + _dot(r_ref[...], wmix_ref[ATTN_W:ATTN_W + R_VW, :]))
    x = x_ref[...] + _rms(m, gmix_ref[...])
    h = _rms(x, gpre_ref[...]).astype(BF16)
    q = (_dot(h, wq_ref[...]) * (X_HEAD_DIM ** -0.5)).astype(BF16)
    outs = []
    for hd in range(X_HEADS):
        sl = slice(hd * X_HEAD_DIM, (hd + 1) * X_HEAD_DIM)
        s = _dot_nt(q[:, sl], mk_ref[:, sl].astype(BF16))
        p = jnp.exp(s - jnp.max(s, axis=-1, keepdims=True))
        den = jnp.sum(p, axis=-1, keepdims=True)
        outs.append((_dot(p.astype(BF16), mv_ref[:, sl].astype(BF16)) / den).astype(BF16))
    y = _dot(jnp.concatenate(outs, axis=1), wo_ref[...])
    o_ref[...] = x + _rms(y, gpost_ref[...])


def _mix_xattn(layer, attn, rec, w_o, x, gmix, gpre, w_xq, mk, mv, w_xo, gpost, seq, tm):
    t = x.shape[0]
    tpb = seq // tm
    grid_spec = pltpu.PrefetchScalarGridSpec(
        num_scalar_prefetch=1, grid=(t // tm,),
        in_specs=[
            pl.BlockSpec((tm, ATTN_W), lambda i, l: (i, 0)),
            pl.BlockSpec((tm, R_VW), lambda i, l: (i, 0)),
            _resident((None, ATTN_W + R_VW, D_MODEL), lambda i, l: (l[0], 0, 0)),
            pl.BlockSpec((tm, D_MODEL), lambda i, l: (i, 0)),
            pl.BlockSpec((1, D_MODEL), lambda i, l: (0, 0)),
            pl.BlockSpec((1, D_MODEL), lambda i, l: (0, 0)),
            _resident((None, D_MODEL, D_MODEL), lambda i, l: (l[0], 0, 0)),
            pl.BlockSpec((None, N_MEM, D_MODEL), lambda i, l: (l[0], i // tpb, 0)),
            pl.BlockSpec((None, N_MEM, D_MODEL), lambda i, l: (l[0], i // tpb, 0)),
            _resident((None, D_MODEL, D_MODEL), lambda i, l: (l[0], 0, 0)),
            pl.BlockSpec((1, D_MODEL), lambda i, l: (0, 0)),
        ],
        out_specs=pl.BlockSpec((tm, D_MODEL), lambda i, l: (i, 0)))
    return pl.pallas_call(_mix_xattn_kernel, grid_spec=grid_spec,
                          out_shape=jax.ShapeDtypeStruct(x.shape, F32),
                          compiler_params=_cparams(1), name="mix_xattn")(
                              layer, attn, rec, w_o, x, gmix, gpre, w_xq, mk, mv, w_xo, gpost)


def _ffn_kernel(*refs, sample, tiles_per_batch):
    if sample:
        (l_ref, x_ref, gpre_ref, wup_ref, cw_ref, cb_ref, wdn_ref, gpost_ref, buf_ref,
         o_ref, tail_ref, act_ref) = refs
    else:
        (l_ref, x_ref, gpre_ref, wup_ref, cw_ref, cb_ref, wdn_ref, gpost_ref,
         o_ref, tail_ref, act_ref, carry_ref) = refs
        i = pl.program_id(0)

        @pl.when(i % tiles_per_batch == 0)
        def _():
            carry_ref[...] = jnp.zeros_like(carry_ref)

    x = x_ref[...]
    tm = x.shape[0]
    h = _rms(x, gpre_ref[...]).astype(BF16)
    sub0 = lax.broadcasted_iota(jnp.int32, (SUBLANES, FC), 0) == 0

    def shift1(a, first):
        rolled = pltpu.roll(a, 1, 0)
        head = jnp.where(sub0, first, rolled[0:SUBLANES])
        return jnp.concatenate([head, rolled[SUBLANES:]], axis=0)

    def conv(u, cs):
        w0 = cw_ref[0:1, cs]
        w1 = cw_ref[1:2, cs]
        w2 = cw_ref[2:3, cs]
        if sample:
            u2 = buf_ref[:, cs]
            u1 = buf_ref[:, slice(2 * D_FF + cs.start, 2 * D_FF + cs.stop)]
            tail_ref[:, cs] = u1
            tail_ref[:, slice(2 * D_FF + cs.start, 2 * D_FF + cs.stop)] = u
            return cb_ref[:, cs] + u2 * w0 + u1 * w1 + u * w2
        p0 = carry_ref[0:1, cs]
        p1 = carry_ref[1:2, cs]
        carry_ref[:, cs] = u[tm - (CONV_W - 1):tm, :]
        v = u * w1 + shift1(u * w0, p1 * w0)
        return cb_ref[:, cs] + u * w2 + shift1(v, p1 * w1 + p0 * w0)

    for j in range(D_FF // FC):
        ca_cols = slice(j * FC, (j + 1) * FC)
        cb_cols = slice(D_FF + j * FC, D_FF + (j + 1) * FC)
        ca = conv(_dot(h, wup_ref[:, ca_cols]), ca_cols)
        cb = conv(_dot(h, wup_ref[:, cb_cols]), cb_cols)
        act_ref[:, ca_cols] = (ca * _sigmoid(ca) * cb).astype(BF16)
    o_ref[...] = x + _rms(_dot(act_ref[...], wdn_ref[...]), gpost_ref[...])

    if not sample:
        @pl.when(i % tiles_per_batch == tiles_per_batch - 1)
        def _():
            tail_ref[0] = carry_ref[...]


def _ffn(layer, x, gpre, w_up, conv_w, conv_b, w_down, gpost, *, seq=None, tm=None, buf=None):
    t = x.shape[0]
    sample = buf is not None
    in_specs = [
        pl.BlockSpec((tm, D_MODEL), lambda i, l: (i, 0)),
        pl.BlockSpec((1, D_MODEL), lambda i, l: (0, 0)),
        _resident((None, D_MODEL, 2 * D_FF), lambda i, l: (l[0], 0, 0)),
        pl.BlockSpec((CONV_W, 2 * D_FF), lambda i, l: (0, 0)),
        pl.BlockSpec((1, 2 * D_FF), lambda i, l: (0, 0)),
        _resident((None, D_FF, D_MODEL), lambda i, l: (l[0], 0, 0)),
        pl.BlockSpec((1, D_MODEL), lambda i, l: (0, 0)),
    ]
    args = [layer, x, gpre, w_up, conv_w, conv_b, w_down, gpost]
    tail_w = (CONV_W - 1) * 2 * D_FF
    if sample:
        tiles_per_batch = 1
        in_specs.append(pl.BlockSpec((None, tm, tail_w), lambda i, l: (l[0], i, 0)))
        args.append(buf)
        tail_spec = pl.BlockSpec((tm, tail_w), lambda i, l: (i, 0))
        tail_shape = jax.ShapeDtypeStruct((t, tail_w), F32)
        scratch = [pltpu.VMEM((tm, D_FF), BF16)]
    else:
        tiles_per_batch = seq // tm
        batch = t // seq
        tail_spec = pl.BlockSpec((1, CONV_W - 1, 2 * D_FF), lambda i, l: (i // tiles_per_batch, 0, 0))
        tail_shape = jax.ShapeDtypeStruct((batch, CONV_W - 1, 2 * D_FF), F32)
        scratch = [pltpu.VMEM((tm, D_FF), BF16), pltpu.VMEM((CONV_W - 1, 2 * D_FF), F32)]
    grid_spec = pltpu.PrefetchScalarGridSpec(
        num_scalar_prefetch=1, grid=(t // tm,), in_specs=in_specs,
        out_specs=[pl.BlockSpec((tm, D_MODEL), lambda i, l: (i, 0)), tail_spec],
        scratch_shapes=scratch)
    body = functools.partial(_ffn_kernel, sample=sample, tiles_per_batch=tiles_per_batch)
    return pl.pallas_call(body, grid_spec=grid_spec,
                          out_shape=(jax.ShapeDtypeStruct(x.shape, F32), tail_shape),
                          compiler_params=_cparams(1),
                          name="ffn_sample" if sample else "ffn")(*args)


def _s_attn_kernel(l_ref, q_ref, kv_ref, ckt_ref, cvt_ref, sink_ref, o_ref):
    bt = q_ref.shape[0]
    for bi in range(bt):
        for g in range(N_KV):
            hs = slice(g * GQA_G, (g + 1) * GQA_G)
            ds = slice(g * HEAD_DIM, (g + 1) * HEAD_DIM)
            qg = q_ref[bi, hs, :]
            k_new = kv_ref[bi:bi + 1, ds]
            v_new = kv_ref[bi:bi + 1, KV_W + g * HEAD_DIM:KV_W + (g + 1) * HEAD_DIM]
            s = _dot(qg.astype(BF16), ckt_ref[bi, g].astype(BF16))
            s_new = jnp.sum(qg * k_new, axis=-1, keepdims=True)
            sk = sink_ref[hs, :]
            m = jnp.maximum(jnp.maximum(jnp.max(s, axis=-1, keepdims=True), s_new), sk)
            p = jnp.exp(s - m)
            p_new = jnp.exp(s_new - m)
            den = jnp.sum(p, axis=-1, keepdims=True) + p_new + jnp.exp(sk - m)
            o = _dot_nt(p.astype(BF16), cvt_ref[bi, g].astype(BF16)) + p_new * v_new
            o_ref[bi, hs, :] = o / den


def _s_attn(layer, q, kv, cache_kt, cache_vt, sink_col):
    n = q.shape[0]
    w = cache_kt.shape[-1]
    grid_spec = pltpu.PrefetchScalarGridSpec(
        num_scalar_prefetch=1, grid=(n // S_BT,),
        in_specs=[
            pl.BlockSpec((S_BT, N_HEADS, HEAD_DIM), lambda i, l: (i, 0, 0)),
            pl.BlockSpec((S_BT, 2 * KV_W), lambda i, l: (i, 0)),
            pl.BlockSpec((None, S_BT, N_KV, HEAD_DIM, w), lambda i, l: (l[0], i, 0, 0, 0)),
            pl.BlockSpec((None, S_BT, N_KV, HEAD_DIM, w), lambda i, l: (l[0], i, 0, 0, 0)),
            pl.BlockSpec((N_HEADS, 1), lambda i, l: (0, 0)),
        ],
        out_specs=pl.BlockSpec((S_BT, N_HEADS, HEAD_DIM), lambda i, l: (i, 0, 0)))
    return pl.pallas_call(_s_attn_kernel, grid_spec=grid_spec,
                          out_shape=jax.ShapeDtypeStruct((n, N_HEADS, HEAD_DIM), F32),
                          compiler_params=_cparams(1), name="s_attn")(
                              layer, q.reshape(n, N_HEADS, HEAD_DIM), kv, cache_kt, cache_vt, sink_col
                          ).reshape(n, ATTN_W)


def _s_hgrn_kernel(l_ref, r_ref, rt_ref, st_ref, lbt_ref, gn_ref, alias_ref, o_ref, so_ref):
    del alias_ref
    bt = r_ref.shape[0]
    for h in range(R_HEADS):
        sl = slice(h * R_DK, (h + 1) * R_DK)
        zt = rt_ref[R_KW + h * R_DK:R_KW + (h + 1) * R_DK, :]
        lbc = lbt_ref[sl, :]
        logf, kt = _forget_and_key(zt, lbc)
        f = jnp.exp(logf)
        qr = r_ref[:, sl]
        q16 = (qr * _sigmoid(qr)).astype(BF16)
        vs = slice(2 * R_KW + h * R_DV, 2 * R_KW + (h + 1) * R_DV)
        gs = slice(2 * R_KW + R_VW + h * R_DV, 2 * R_KW + R_VW + (h + 1) * R_DV)
        s_new = [f[:, bi:bi + 1] * st_ref[bi, h] + kt[:, bi:bi + 1] * r_ref[bi:bi + 1, vs]
                 for bi in range(bt)]
        for bi in range(bt):
            so_ref[bi, h] = s_new[bi]
        o = jnp.concatenate([_dot(q16[bi:bi + 1, :], s_new[bi].astype(BF16))
                             for bi in range(bt)], axis=0)
        on = o * lax.rsqrt(jnp.mean(o * o, axis=-1, keepdims=True) + EPS) * gn_ref[:, sl]
        gr = r_ref[:, gs]
        o_ref[:, sl] = (on * (gr * _sigmoid(gr))).astype(o_ref.dtype)


def _s_hgrn(layer, r, state, lb_col, gn, new_states):
    n = r.shape[0]
    rt = r.reshape(n // S_BT, S_BT, R_IN_W).transpose(0, 2, 1)
    state_spec = pl.BlockSpec((None, S_BT, R_HEADS, R_DK, R_DV), lambda i, l: (l[0], i, 0, 0, 0))
    grid_spec = pltpu.PrefetchScalarGridSpec(
        num_scalar_prefetch=1, grid=(n // S_BT,),
        in_specs=[
            pl.BlockSpec((S_BT, R_IN_W), lambda i, l: (i, 0)),
            pl.BlockSpec((None, R_IN_W, S_BT), lambda i, l: (i, 0, 0)),
            state_spec,
            pl.BlockSpec((R_KW, 1), lambda i, l: (0, 0)),
            pl.BlockSpec((1, R_VW), lambda i, l: (0, 0)),
            pl.BlockSpec(memory_space=pl.ANY),
        ],
        out_specs=[pl.BlockSpec((S_BT, R_VW), lambda i, l: (i, 0)), state_spec])
    return pl.pallas_call(_s_hgrn_kernel, grid_spec=grid_spec,
                          out_shape=(jax.ShapeDtypeStruct((n, R_VW), F32),
                                     jax.ShapeDtypeStruct(new_states.shape, F32)),
                          input_output_aliases={6: 1},
                          compiler_params=_cparams(1), name="s_hgrn")(
                              layer, r, rt, state, lb_col, gn, new_states)


def _s_xattn_kernel(l_ref, x_ref, gpre_ref, wq_ref, ck_ref, cv_ref, wo_ref, gpost_ref, o_ref):
    bt = ck_ref.shape[0]
    x = x_ref[...]
    h = _rms(x, gpre_ref[...]).astype(BF16)
    q = _dot(h, wq_ref[...]) * (X_HEAD_DIM ** -0.5)
    rows = []
    for bi in range(bt):
        qb = jnp.concatenate([q[bi:bi + 1, hd * X_HEAD_DIM:(hd + 1) * X_HEAD_DIM]
                              for hd in range(X_HEADS)], axis=0)
        s = jnp.sum(ck_ref[bi] * qb[None], axis=-1, keepdims=True)
        p = jnp.exp(s - jnp.max(s, axis=0, keepdims=True))
        den = jnp.sum(p, axis=0)
        o = jnp.sum(p * cv_ref[bi], axis=0) / den
        rows.append(jnp.concatenate([o[hd:hd + 1, :] for hd in range(X_HEADS)], axis=1))
    y = _dot(jnp.concatenate(rows, axis=0).astype(BF16), wo_ref[...])
    o_ref[...] = x + _rms(y, gpost_ref[...])


def _s_xattn(layer, x, gpre, w_xq, cache_k, cache_v, w_xo, gpost):
    n = x.shape[0]
    grid_spec = pltpu.PrefetchScalarGridSpec(
        num_scalar_prefetch=1, grid=(n // X_BT,),
        in_specs=[
            pl.BlockSpec((X_BT, D_MODEL), lambda i, l: (i, 0)),
            pl.BlockSpec((1, D_MODEL), lambda i, l: (0, 0)),
            _resident((None, D_MODEL, D_MODEL), lambda i, l: (l[0], 0, 0)),
            pl.BlockSpec((None, X_BT, N_MEM, X_HEADS, X_HEAD_DIM), lambda i, l: (l[0], i, 0, 0, 0)),
            pl.BlockSpec((None, X_BT, N_MEM, X_HEADS, X_HEAD_DIM), lambda i, l: (l[0], i, 0, 0, 0)),
            _resident((None, D_MODEL, D_MODEL), lambda i, l: (l[0], 0, 0)),
            pl.BlockSpec((1, D_MODEL), lambda i, l: (0, 0)),
        ],
        out_specs=pl.BlockSpec((X_BT, D_MODEL), lambda i, l: (i, 0)))
    return pl.pallas_call(_s_xattn_kernel, grid_spec=grid_spec,
                          out_shape=jax.ShapeDtypeStruct(x.shape, F32),
                          compiler_params=_cparams(1), name="s_xattn")(
                              layer, x, gpre, w_xq, cache_k, cache_v, w_xo, gpost)


def kernel(x_prompt, x_sample, cache_win_k, cache_win_v, cache_mem_k, cache_mem_v, state_hgrn, cache_ffn_conv, mem_prompt, w_in, w_o, attn_sinks, lb_logits, hgrn_norm, w_xq, w_xk, w_xv, w_xo, w_up, conv_w, conv_b, w_down, g_pre_mix, g_post_mix, g_pre_x, g_post_x, g_mem, g_pre_ffn, g_post_ffn):
    batch, seq, _ = x_prompt.shape
    n_s, dec_seq, _ = x_sample.shape
    depth = w_in.shape[0]
    assert dec_seq == 1 and seq % WINDOW == 0 and seq % HC == 0
    assert cache_win_k.shape[2] == WINDOW and n_s % S_BT == 0 and n_s % X_BT == 0
    tm = min(TM, seq)
    assert seq % tm == 0

    w_in_b, w_o_b, w_xq_b, w_xk_b, w_xv_b, w_xo_b, w_up_b, w_down_b = (
        w.astype(BF16) for w in (w_in, w_o, w_xq, w_xk, w_xv, w_xo, w_up, w_down))
    lb, cos_p, sin_p, cos_s, sin_s = _prep(lb_logits, seq, n_s)
    mk, mv = _memkv(mem_prompt.reshape(batch * N_MEM, D_MODEL), g_mem.reshape(depth, 1, D_MODEL),
                    w_xk_b, w_xv_b)

    cwk = jnp.transpose(cache_win_k, (0, 1, 3, 4, 2))
    cwv = jnp.transpose(cache_win_v, (0, 1, 3, 4, 2))
    cmk, cmv = cache_mem_k, cache_mem_v
    cfc = cache_ffn_conv.reshape(depth, n_s, (CONV_W - 1) * 2 * D_FF)

    def layer_fn(carry, l):
        xp, xs, st_s = carry
        layer = jnp.reshape(l, (1,)).astype(jnp.int32)

        def vec(a):
            return lax.dynamic_index_in_dim(a, l, 0, keepdims=True)

        sink, lb_l, gn_l = vec(attn_sinks), vec(lb), vec(hgrn_norm)
        cw_l, cb_l = lax.dynamic_index_in_dim(conv_w, l, 0, keepdims=False), vec(conv_b)
        gpm, gom, gpx, gox, gpf, gof = (vec(g) for g in (g_pre_mix, g_post_mix, g_pre_x, g_post_x,
                                                        g_pre_ffn, g_post_ffn))

        q_p, kv_p, r_p = _proj_in(layer, xp, gpm, w_in_b, cos_p, sin_p, tm, BF16)
        attn_p = _swa(q_p, kv_p, sink, seq)
        rec_p, st_p = _hgrn(r_p, lb_l, gn_l, batch, seq)
        xp = _mix_xattn(layer, attn_p, rec_p, w_o_b, xp, gom, gpx, w_xq_b, mk, mv, w_xo_b, gox, seq, tm)

        q_s, kv_s, r_s = _proj_in(layer, xs, gpm, w_in_b, cos_s, sin_s, n_s, F32)
        attn_s = _s_attn(layer, q_s, kv_s, cwk, cwv, sink.reshape(N_HEADS, 1))
        rec_s, st_s = _s_hgrn(layer, r_s, state_hgrn, lb_l.reshape(R_KW, 1), gn_l, st_s)
        xs = _mix_out(layer, attn_s, rec_s, w_o_b, xs, gom, n_s)
        xs = _s_xattn(layer, xs, gpx, w_xq_b, cmk, cmv, w_xo_b, gox)

        xp, tail_p = _ffn(layer, xp, gpf, w_up_b, cw_l, cb_l, w_down_b, gof, seq=seq, tm=tm)
        xs, tail_s = _ffn(layer, xs, gpf, w_up_b, cw_l, cb_l, w_down_b, gof, tm=n_s, buf=cfc)

        kv_keep = kv_p.reshape(batch, seq, 2 * KV_W)[:, seq - WINDOW:, :]
        ys = (kv_keep[..., :KV_W], kv_keep[..., KV_W:], kv_s[:, :KV_W], kv_s[:, KV_W:],
              st_p, tail_p, tail_s)
        return (xp, xs, st_s), ys

    init = (x_prompt.reshape(batch * seq, D_MODEL), x_sample.reshape(n_s, D_MODEL),
            jnp.zeros(state_hgrn.shape, F32))
    (xp, xs, st_s), ys = lax.scan(layer_fn, init, jnp.arange(depth, dtype=jnp.int32))
    wk_p, wv_p, wk_s, wv_s, st_p, tail_p, tail_s = ys
    return (xp.reshape(batch, seq, D_MODEL), xs.reshape(n_s, 1, D_MODEL),
            wk_p.reshape(depth, batch, WINDOW, N_KV, HEAD_DIM), wv_p.reshape(depth, batch, WINDOW, N_KV, HEAD_DIM),
            wk_s.reshape(depth, n_s, 1, N_KV, HEAD_DIM), wv_s.reshape(depth, n_s, 1, N_KV, HEAD_DIM),
            mk.reshape(depth, batch, N_MEM, X_HEADS, X_HEAD_DIM), mv.reshape(depth, batch, N_MEM, X_HEADS, X_HEAD_DIM),
            st_p, st_s,
            tail_p, tail_s.reshape(depth, n_s, CONV_W - 1, 2 * D_FF))
```

```python
import functools
import math

import jax
import jax.numpy as jnp
from jax import lax
from jax.experimental import pallas as pl
from jax.experimental.pallas import tpu as pltpu

F32 = jnp.float32
BF16 = jnp.bfloat16

D_MODEL = 1024
HEAD_DIM = 64
N_HEADS = 8
N_KV = 2
GQA_G = N_HEADS // N_KV
WINDOW = 128
ROPE_THETA = 10000.0
R_HEADS = 4
R_DK = 128
R_DV = 128
ATTN_W = N_HEADS * HEAD_DIM
KV_W = N_KV * HEAD_DIM
R_KW = R_HEADS * R_DK
R_VW = R_HEADS * R_DV
R_IN_W = 2 * R_KW + 2 * R_VW
P_IN = ATTN_W + 2 * KV_W + R_IN_W
N_MEM = 256
X_HEADS = 4
X_HEAD_DIM = D_MODEL // X_HEADS
X_ROWS = X_HEADS * (X_HEAD_DIM // 128)
D_FF = 2816
CONV_W = 3
EPS = 1e-6
PAST_LEN = 8192

LANES = 128
SUBLANES = 8
VMEM_LIMIT = 56 * 1024 * 1024

TM = 512
SWA_TQ = 512
HC = 128
HG_ROWS = 512
LOG2_E = 1.4426950408889634
FC = 256
S_BT = 8
X_BT = 8


def _cparams(n_axes):
    return pltpu.CompilerParams(dimension_semantics=("arbitrary",) * n_axes,
                                vmem_limit_bytes=VMEM_LIMIT)


def _rms(x, g):
    return x * lax.rsqrt(jnp.mean(x * x, axis=-1, keepdims=True) + EPS) * g


def _sigmoid(x):
    return 1.0 / (1.0 + jnp.exp(-x))


def _dot(a, b):
    return jnp.dot(a, b, preferred_element_type=F32)


def _dot_nt(a, b):
    return lax.dot_general(a, b, (((1,), (1,)), ((), ())), preferred_element_type=F32)


def _dot_tn(a, b):
    return lax.dot_general(a, b, (((0,), (0,)), ((), ())), preferred_element_type=F32)


def _resident(block_shape, index_map):
    return pl.BlockSpec(block_shape, index_map, pipeline_mode=pl.Buffered(1))


def _prep_kernel(lbl_ref, lb_ref, cos_ref, sin_ref, cos_s_ref, sin_s_ref):
    x = lbl_ref[...]
    e = jnp.exp(x - jnp.max(x, axis=0, keepdims=True))
    sm = e / jnp.sum(e, axis=0, keepdims=True)
    depth = x.shape[0]
    acc = jnp.zeros((1, x.shape[1]), F32)
    rows = [acc]
    for l in range(1, depth):
        acc = acc + sm[l:l + 1, :]
        rows.append(acc)
    lb_ref[...] = jnp.concatenate(rows, axis=0)

    half = HEAD_DIM // 2

    def tables(shape, pos):
        lane = lax.broadcasted_iota(jnp.int32, shape, 1)
        j = (lane & (half - 1)).astype(F32)
        inv_freq = jnp.exp(j * (-math.log(ROPE_THETA) / half))
        ang = pos * inv_freq
        first = (lane & (HEAD_DIM - 1)) < half
        return jnp.cos(ang), jnp.where(first, -jnp.sin(ang), jnp.sin(ang))

    pos_p = lax.broadcasted_iota(jnp.int32, cos_ref.shape, 0).astype(F32)
    c, s = tables(cos_ref.shape, pos_p)
    cos_ref[...] = c
    sin_ref[...] = s
    c, s = tables(cos_s_ref.shape, jnp.full(cos_s_ref.shape, float(PAST_LEN), F32))
    cos_s_ref[...] = c
    sin_s_ref[...] = s


def _prep(lb_logits, seq, n_sample):
    depth = lb_logits.shape[0]
    out_shape = (jax.ShapeDtypeStruct((depth, R_KW), F32),
                 jax.ShapeDtypeStruct((seq, LANES), F32), jax.ShapeDtypeStruct((seq, LANES), F32),
                 jax.ShapeDtypeStruct((n_sample, LANES), F32), jax.ShapeDtypeStruct((n_sample, LANES), F32))
    return pl.pallas_call(_prep_kernel, out_shape=out_shape, name="prep")(lb_logits)


def _rope(x, cos, sin):
    w = x.shape[-1]
    reps = w // LANES
    if reps > 1:
        cos = jnp.tile(cos, (1, reps))
        sin = jnp.tile(sin, (1, reps))
    half = HEAD_DIM // 2
    lane = lax.broadcasted_iota(jnp.int32, x.shape, 1)
    first = (lane & (HEAD_DIM - 1)) < half
    swapped = jnp.where(first, pltpu.roll(x, w - half, 1), pltpu.roll(x, half, 1))
    return x * cos + swapped * sin


def _proj_in_kernel(l_ref, x_ref, g_ref, w_ref, cos_ref, sin_ref, q_ref, kv_ref, r_ref):
    h = _rms(x_ref[...], g_ref[...]).astype(BF16)
    cos = cos_ref[...]
    sin = sin_ref[...]
    q = _dot(h, w_ref[:, 0:ATTN_W])
    q_ref[...] = (_rope(q, cos, sin) * (HEAD_DIM ** -0.5)).astype(q_ref.dtype)
    k = _dot(h, w_ref[:, ATTN_W:ATTN_W + KV_W])
    kv_ref[:, 0:KV_W] = _rope(k, cos, sin)
    kv_ref[:, KV_W:2 * KV_W] = _dot(h, w_ref[:, ATTN_W + KV_W:ATTN_W + 2 * KV_W])
    r_ref[...] = _dot(h, w_ref[:, ATTN_W + 2 * KV_W:P_IN])


def _proj_in(layer, x, g, w_in, cos, sin, tm, q_dtype):
    t = x.shape[0]
    n_tab = cos.shape[0] // tm
    grid_spec = pltpu.PrefetchScalarGridSpec(
        num_scalar_prefetch=1, grid=(t // tm,),
        in_specs=[
            pl.BlockSpec((tm, D_MODEL), lambda i, l: (i, 0)),
            pl.BlockSpec((1, D_MODEL), lambda i, l: (0, 0)),
            _resident((None, D_MODEL, P_IN), lambda i, l: (l[0], 0, 0)),
            pl.BlockSpec((tm, LANES), lambda i, l: (i % n_tab, 0)),
            pl.BlockSpec((tm, LANES), lambda i, l: (i % n_tab, 0)),
        ],
        out_specs=[
            pl.BlockSpec((tm, ATTN_W), lambda i, l: (i, 0)),
            pl.BlockSpec((tm, 2 * KV_W), lambda i, l: (i, 0)),
            pl.BlockSpec((tm, R_IN_W), lambda i, l: (i, 0)),
        ])
    out_shape = (jax.ShapeDtypeStruct((t, ATTN_W), q_dtype),
                 jax.ShapeDtypeStruct((t, 2 * KV_W), F32),
                 jax.ShapeDtypeStruct((t, R_IN_W), F32))
    return pl.pallas_call(_proj_in_kernel, grid_spec=grid_spec, out_shape=out_shape,
                          compiler_params=_cparams(1), name="proj_in")(layer, x, g, w_in, cos, sin)


def _swa_kernel(q_ref, kv_ref, kvp_ref, sink_ref, o_ref, *, tiles_per_batch):
    i = pl.program_id(0)
    w = WINDOW
    nblk = q_ref.shape[0] // w
    first_key = jnp.where(i % tiles_per_batch == 0, w, 0)
    row = lax.broadcasted_iota(jnp.int32, (w, 2 * w), 0)
    col = lax.broadcasted_iota(jnp.int32, (w, 2 * w), 1)
    band = (col >= row) & (col <= row + w)
    low_o = lax.broadcasted_iota(jnp.int32, (w, LANES), 1) < HEAD_DIM
    low_v = lax.broadcasted_iota(jnp.int32, (2 * w, LANES), 1) < HEAD_DIM
    zeros_half = jnp.zeros((HEAD_DIM, 2 * w), BF16)
    for b in range(nblk):
        if b == 0:
            kv2 = jnp.concatenate([kvp_ref[...], kv_ref[0:w, :]], axis=0)
            valid = band & (col >= first_key)
        else:
            kv2 = kv_ref[(b - 1) * w:(b + 1) * w, :]
            valid = band
        kt = kv2[:, 0:KV_W].T.astype(BF16)
        v2 = kv2[:, KV_W:2 * KV_W]
        v_sw = pltpu.roll(v2, HEAD_DIM, 1)
        qb = q_ref[b * w:(b + 1) * w, :]
        k_pad, vcat = [], []
        for g in range(N_KV):
            kg = kt[g * HEAD_DIM:(g + 1) * HEAD_DIM, :]
            k_pad.append((jnp.concatenate([kg, zeros_half], axis=0), jnp.concatenate([zeros_half, kg], axis=0)))
            if g == 0:
                v_lo, v_hi = jnp.where(low_v, v2, 0.0), jnp.where(low_v, 0.0, v_sw)
            else:
                v_lo, v_hi = jnp.where(low_v, v_sw, 0.0), jnp.where(low_v, 0.0, v2)
            vcat.append(jnp.concatenate([v_lo, v_hi], axis=0).astype(BF16))
        scores = [jnp.where(valid, _dot(qb[:, (h // 2) * LANES:(h // 2 + 1) * LANES], k_pad[h // GQA_G][h % 2]),
                            -jnp.inf) for h in range(N_HEADS)]
        sinks = [sink_ref[0:1, h:h + 1] for h in range(N_HEADS)]
        maxes = [jnp.maximum(jnp.max(s, axis=-1, keepdims=True), sk) for s, sk in zip(scores, sinks)]
        probs = [jnp.exp(s - m) for s, m in zip(scores, maxes)]
        rden = [1.0 / (jnp.sum(p, axis=-1, keepdims=True) + jnp.exp(sk - m))
                for p, sk, m in zip(probs, sinks, maxes)]
        for pr in range(N_HEADS // 2):
            ps = slice(pr * LANES, (pr + 1) * LANES)
            pcat = jnp.concatenate([probs[2 * pr].astype(BF16), probs[2 * pr + 1].astype(BF16)], axis=1)
            o = _dot(pcat, vcat[(2 * pr) // GQA_G])
            o_ref[b * w:(b + 1) * w, ps] = (o * jnp.where(low_o, rden[2 * pr], rden[2 * pr + 1])).astype(o_ref.dtype)


def _swa(q, kv, sink, seq):
    t = q.shape[0]
    tq = min(SWA_TQ, seq)
    nblk = tq // WINDOW
    body = functools.partial(_swa_kernel, tiles_per_batch=seq // tq)
    return pl.pallas_call(
        body, grid=(t // tq,),
        in_specs=[
            pl.BlockSpec((tq, ATTN_W), lambda i: (i, 0)),
            pl.BlockSpec((tq, 2 * KV_W), lambda i: (i, 0)),
            pl.BlockSpec((WINDOW, 2 * KV_W), lambda i: (jnp.maximum(i * nblk - 1, 0), 0)),
            pl.BlockSpec((1, N_HEADS), lambda i: (0, 0)),
        ],
        out_specs=pl.BlockSpec((tq, ATTN_W), lambda i: (i, 0)),
        out_shape=jax.ShapeDtypeStruct(q.shape, BF16),
        compiler_params=_cparams(1), name="swa")(q, kv, kv, sink)


def _forget_and_key(z, lb):
    e = jnp.exp(-jnp.abs(z))
    logsig = jnp.minimum(z, 0.0) - jnp.log(1.0 + e)
    a = jnp.log(lb)
    c = jnp.log(1.0 - lb) + logsig
    logf = jnp.maximum(a, c) + jnp.log(1.0 + jnp.exp(-jnp.abs(a - c)))
    key = (1.0 - lb) * (jnp.where(z >= 0.0, e, 1.0) / (1.0 + e))
    return logf, key


def _cumsum_rows(g):
    n = g.shape[0]
    row = lax.broadcasted_iota(jnp.int32, g.shape, 0)
    b = g
    sh = 1
    while sh < n:
        if sh % SUBLANES == 0:
            shifted = jnp.concatenate([jnp.zeros((sh, g.shape[1]), F32), b[:n - sh]], axis=0)
        else:
            shifted = jnp.where(row >= sh, pltpu.roll(b, sh, 0), 0.0)
        b = b + shifted
        sh *= 2
    return b


def _level_ref(b, half):
    n_rows, width = b.shape
    n = 2 * half
    if n >= 2 * SUBLANES:
        pieces = [jnp.broadcast_to(b[i * n + half - 1:i * n + half, :], (n, width))
                  for i in range(n_rows // n)]
        return pieces[0] if len(pieces) == 1 else jnp.concatenate(pieces, axis=0)
    b3 = b.reshape(n_rows // SUBLANES, SUBLANES, width)
    sub = lax.broadcasted_iota(jnp.int32, b3.shape, 1)

    def bcast(r):
        return jnp.broadcast_to(b3[:, r:r + 1, :], b3.shape)

    if half == 4:
        ref = bcast(3)
    elif half == 2:
        ref = jnp.where(sub < 4, bcast(1), bcast(5))
    else:
        ref = jnp.where(sub < 2, bcast(0), jnp.where(sub < 4, bcast(2), jnp.where(sub < 6, bcast(4), bcast(6))))
    return ref.reshape(n_rows, width)


def _hgrn_kernel(qr_ref, fr_ref, ir_ref, gr_ref, lb_ref, gn_ref, o_ref, s_out_ref, st_ref):
    c = pl.program_id(1)
    last = pl.num_programs(1) - 1

    @pl.when(c == 0)
    def _():
        st_ref[...] = jnp.zeros_like(st_ref)

    rr = lax.broadcasted_iota(jnp.int32, (HC, HC), 0)
    cc = lax.broadcasted_iota(jnp.int32, (HC, HC), 1)
    xor = rr ^ cc
    causal = cc <= rr
    halves = [1 << i for i in range(HC.bit_length() - 1)]
    row_w = lax.broadcasted_iota(jnp.int32, (HC, R_KW), 0)
    signs = [jnp.where((row_w & half) != 0, 1.0, -1.0) for half in halves]
    heads = [slice(h * R_DK, (h + 1) * R_DK) for h in range(R_HEADS)]
    st = [st_ref[h] for h in range(R_HEADS)]

    for ci in range(qr_ref.shape[0] // HC):
        rows = slice(ci * HC, (ci + 1) * HC)
        g, k = _forget_and_key(fr_ref[rows, :], lb_ref[...])
        qr = qr_ref[rows, :]
        q = qr * _sigmoid(qr)
        v16 = ir_ref[rows, :].astype(BF16)
        b = _cumsum_rows(g * LOG2_E)
        q16 = q.astype(BF16)
        k16 = k.astype(BF16)
        b_last = b[HC - 1:HC, :]
        q_in = (q * jnp.exp2(b)).astype(BF16)
        k_out = (k * jnp.exp2(b_last - b)).astype(BF16)
        decay = jnp.exp2(b_last)
        gr = gr_ref[rows, :]
        gate = gr * _sigmoid(gr)
        q_lv, k_lv = [], []
        for half, sgn in zip(halves, signs):
            e = jnp.exp2((b - _level_ref(b, half)) * sgn).astype(BF16)
            q_lv.append(q16 * e)
            k_lv.append(k16 * e)

        diag = [_dot_nt(q16[:, sl], k16[:, sl]) for sl in heads]
        levels = [[_dot_nt(ql[:, sl], kl[:, sl]) for sl in heads] for ql, kl in zip(q_lv, k_lv)]
        att16 = []
        for h in range(R_HEADS):
            att = diag[h]
            for half, lv in zip(halves, levels):
                att = jnp.where(xor >= half, lv[h], att)
            att16.append(jnp.where(causal, att, 0.0).astype(BF16))
        outs = [_dot(att16[h], v16[:, sl]) + _dot_nt(q_in[:, sl], st[h].astype(BF16))
                for h, sl in enumerate(heads)]
        st = [st[h] * decay[:, sl] + _dot_tn(v16[:, sl], k_out[:, sl]) for h, sl in enumerate(heads)]
        for h, sl in enumerate(heads):
            o = outs[h]
            on = o * lax.rsqrt(jnp.mean(o * o, axis=-1, keepdims=True) + EPS) * gn_ref[:, sl]
            o_ref[rows, sl] = (on * gate[:, sl]).astype(o_ref.dtype)

    for h in range(R_HEADS):
        st_ref[h] = st[h]

    @pl.when(c == last)
    def _():
        for h in range(R_HEADS):
            s_out_ref[0, h] = st[h].T


def _hgrn(r, lb, gn, batch, seq):
    rows = min(HG_ROWS, seq)
    nc = seq // rows
    t = r.shape[0]

    def col(kk):
        return pl.BlockSpec((rows, R_KW), lambda b, c: (b * nc + c, kk))

    return pl.pallas_call(
        _hgrn_kernel, grid=(batch, nc),
        in_specs=[col(0), col(1), col(2), col(3),
                  pl.BlockSpec((1, R_KW), lambda b, c: (0, 0)),
                  pl.BlockSpec((1, R_VW), lambda b, c: (0, 0))],
        out_specs=[pl.BlockSpec((rows, R_VW), lambda b, c: (b * nc + c, 0)),
                   pl.BlockSpec((1, R_HEADS, R_DK, R_DV), lambda b, c: (b, 0, 0, 0))],
        out_shape=(jax.ShapeDtypeStruct((t, R_VW), BF16),
                   jax.ShapeDtypeStruct((batch, R_HEADS, R_DK, R_DV), F32)),
        scratch_shapes=[pltpu.VMEM((R_HEADS, R_DV, R_DK), F32)],
        compiler_params=_cparams(2), name="hgrn")(r, r, r, r, lb, gn)


def _mix_out_kernel(l_ref, a_ref, r_ref, w_ref, x_ref, g_ref, o_ref):
    m = (_dot(a_ref[...].astype(BF16), w_ref[0:ATTN_W, :])
         + _dot(r_ref[...].astype(BF16), w_ref[ATTN_W:ATTN_W + R_VW, :]))
    o_ref[...] = x_ref[...] + _rms(m, g_ref[...])


def _mix_out(layer, attn, rec, w_o, x, g, tm):
    t = x.shape[0]
    grid_spec = pltpu.PrefetchScalarGridSpec(
        num_scalar_prefetch=1, grid=(t // tm,),
        in_specs=[
            pl.BlockSpec((tm, ATTN_W), lambda i, l: (i, 0)),
            pl.BlockSpec((tm, R_VW), lambda i, l: (i, 0)),
            _resident((None, ATTN_W + R_VW, D_MODEL), lambda i, l: (l[0], 0, 0)),
            pl.BlockSpec((tm, D_MODEL), lambda i, l: (i, 0)),
            pl.BlockSpec((1, D_MODEL), lambda i, l: (0, 0)),
        ],
        out_specs=pl.BlockSpec((tm, D_MODEL), lambda i, l: (i, 0)))
    return pl.pallas_call(_mix_out_kernel, grid_spec=grid_spec,
                          out_shape=jax.ShapeDtypeStruct(x.shape, F32),
                          compiler_params=_cparams(1), name="mix_out")(layer, attn, rec, w_o, x, g)


def _memkv_kernel(m_ref, g_ref, wk_ref, wv_ref, k_ref, v_ref, k16_ref, v16_ref):
    h = _rms(m_ref[...], g_ref[...]).astype(BF16)
    for w_ref, o_ref, o16_ref in ((wk_ref, k_ref, k16_ref), (wv_ref, v_ref, v16_ref)):
        y = _dot(h, w_ref[...])
        o16_ref[...] = y.astype(BF16)
        for hd in range(X_HEADS):
            o_ref[:, hd, :] = y[:, hd * X_HEAD_DIM:(hd + 1) * X_HEAD_DIM]


def _memkv(mem, g_mem, w_xk, w_xv):
    depth = w_xk.shape[0]
    rows = mem.shape[0]
    tm = min(TM, rows)
    out = jax.ShapeDtypeStruct((depth, rows, X_HEADS, X_HEAD_DIM), F32)
    out16 = jax.ShapeDtypeStruct((depth, rows, D_MODEL), BF16)
    return pl.pallas_call(
        _memkv_kernel, grid=(depth, rows // tm),
        in_specs=[
            pl.BlockSpec((tm, D_MODEL), lambda l, i: (i, 0)),
            pl.BlockSpec((None, 1, D_MODEL), lambda l, i: (l, 0, 0)),
            pl.BlockSpec((None, D_MODEL, D_MODEL), lambda l, i: (l, 0, 0)),
            pl.BlockSpec((None, D_MODEL, D_MODEL), lambda l, i: (l, 0, 0)),
        ],
        out_specs=[pl.BlockSpec((None, tm, X_HEADS, X_HEAD_DIM), lambda l, i: (l, i, 0, 0)),
                   pl.BlockSpec((None, tm, X_HEADS, X_HEAD_DIM), lambda l, i: (l, i, 0, 0)),
                   pl.BlockSpec((None, tm, D_MODEL), lambda l, i: (l, i, 0)),
                   pl.BlockSpec((None, tm, D_MODEL), lambda l, i: (l, i, 0))],
        out_shape=(out, out, out16, out16), compiler_params=_cparams(2), name="memkv")(mem, g_mem, w_xk, w_xv)


def _mix_xattn_kernel(l_ref, a_ref, r_ref, wmix_ref, x_ref, gmix_ref,
                      gpre_ref, wq_ref, mk_ref, mv_ref, wo_ref, gpost_ref, o_ref):
    m = (_dot(a_ref[...], wmix_ref[0:ATTN_W, :]) + _dot(r_ref[...], wmix_ref[ATTN_W:ATTN_W + R_VW, :]))
    x = x_ref[...] + _rms(m, gmix_ref[...])
    h = _rms(x, gpre_ref[...]).astype(BF16)
    q = (_dot(h, wq_ref[...]) * (X_HEAD_DIM ** -0.5)).astype(BF16)
    heads = [slice(hd * X_HEAD_DIM, (hd + 1) * X_HEAD_DIM) for hd in range(X_HEADS)]
    s = [_dot_nt(q[:, sl], mk_ref[:, sl]) for sl in heads]
    p = [jnp.exp(a - jnp.max(a, axis=-1, keepdims=True)) for a in s]
    den = [jnp.sum(a, axis=-1, keepdims=True) for a in p]
    outs = [(_dot(a.astype(BF16), mv_ref[:, sl]) / d).astype(BF16) for a, d, sl in zip(p, den, heads)]
    y = _dot(jnp.concatenate(outs, axis=1), wo_ref[...])
    o_ref[...] = x + _rms(y, gpost_ref[...])


def _mix_xattn(layer, attn, rec, w_o, x, gmix, gpre, w_xq, mk, mv, w_xo, gpost, seq, tm):
    t = x.shape[0]
    tpb = seq // tm
    grid_spec = pltpu.PrefetchScalarGridSpec(
        num_scalar_prefetch=1, grid=(t // tm,),
        in_specs=[
            pl.BlockSpec((tm, ATTN_W), lambda i, l: (i, 0)),
            pl.BlockSpec((tm, R_VW), lambda i, l: (i, 0)),
            _resident((None, ATTN_W + R_VW, D_MODEL), lambda i, l: (l[0], 0, 0)),
            pl.BlockSpec((tm, D_MODEL), lambda i, l: (i, 0)),
            pl.BlockSpec((1, D_MODEL), lambda i, l: (0, 0)),
            pl.BlockSpec((1, D_MODEL), lambda i, l: (0, 0)),
            _resident((None, D_MODEL, D_MODEL), lambda i, l: (l[0], 0, 0)),
            pl.BlockSpec((None, N_MEM, D_MODEL), lambda i, l: (l[0], i // tpb, 0)),
            pl.BlockSpec((None, N_MEM, D_MODEL), lambda i, l: (l[0], i // tpb, 0)),
            _resident((None, D_MODEL, D_MODEL), lambda i, l: (l[0], 0, 0)),
            pl.BlockSpec((1, D_MODEL), lambda i, l: (0, 0)),
        ],
        out_specs=pl.BlockSpec((tm, D_MODEL), lambda i, l: (i, 0)))
    return pl.pallas_call(_mix_xattn_kernel, grid_spec=grid_spec,
                          out_shape=jax.ShapeDtypeStruct(x.shape, F32),
                          compiler_params=_cparams(1), name="mix_xattn")(
                              layer, attn, rec, w_o, x, gmix, gpre, w_xq, mk, mv, w_xo, gpost)


def _conv_chunk(u, cols, cw_ref, cb_ref, carry_ref, sub0):
    tm = u.shape[0]
    w0 = cw_ref[0:1, cols]
    w1 = cw_ref[1:2, cols]
    w2 = cw_ref[2:3, cols]

    def shift1(a, first):
        rolled = pltpu.roll(a, 1, 0)
        head = jnp.where(sub0, first, rolled[0:SUBLANES])
        return jnp.concatenate([head, rolled[SUBLANES:]], axis=0)

    p0 = carry_ref[0:1, cols]
    p1 = carry_ref[1:2, cols]
    carry_ref[:, cols] = u[tm - (CONV_W - 1):tm, :]
    v = u * w1 + shift1(u * w0, p1 * w0)
    return cb_ref[:, cols] + u * w2 + shift1(v, p1 * w1 + p0 * w0)


def _ffn_kernel(l_ref, x_ref, gpre_ref, wup_ref, cw_ref, cb_ref, wdn_ref, gpost_ref,
                o_ref, tail_ref, h_ref, act_ref, carry_ref, *, tiles_per_batch):
    i = pl.program_id(0)

    @pl.when(i % tiles_per_batch == 0)
    def _():
        carry_ref[...] = jnp.zeros_like(carry_ref)

    h_ref[...] = _rms(x_ref[...], gpre_ref[...]).astype(BF16)
    sub0 = lax.broadcasted_iota(jnp.int32, (SUBLANES, FC), 0) == 0
    for j in range(D_FF // FC):
        ca_cols = slice(j * FC, (j + 1) * FC)
        cb_cols = slice(D_FF + j * FC, D_FF + (j + 1) * FC)
        ca = _conv_chunk(_dot(h_ref[...], wup_ref[:, ca_cols]), ca_cols, cw_ref, cb_ref, carry_ref, sub0)
        cb = _conv_chunk(_dot(h_ref[...], wup_ref[:, cb_cols]), cb_cols, cw_ref, cb_ref, carry_ref, sub0)
        act_ref[:, ca_cols] = (ca * _sigmoid(ca) * cb).astype(BF16)
    o_ref[...] = x_ref[...] + _rms(_dot(act_ref[...], wdn_ref[...]), gpost_ref[...])

    @pl.when(i % tiles_per_batch == tiles_per_batch - 1)
    def _():
        tail_ref[0] = carry_ref[...]


def _ffn(layer, x, gpre, w_up, conv_w, conv_b, w_down, gpost, seq, tm):
    t = x.shape[0]
    tiles_per_batch = seq // tm
    grid_spec = pltpu.PrefetchScalarGridSpec(
        num_scalar_prefetch=1, grid=(t // tm,),
        in_specs=[
            pl.BlockSpec((tm, D_MODEL), lambda i, l: (i, 0)),
            pl.BlockSpec((1, D_MODEL), lambda i, l: (0, 0)),
            _resident((None, D_MODEL, 2 * D_FF), lambda i, l: (l[0], 0, 0)),
            pl.BlockSpec((CONV_W, 2 * D_FF), lambda i, l: (0, 0)),
            pl.BlockSpec((1, 2 * D_FF), lambda i, l: (0, 0)),
            _resident((None, D_FF, D_MODEL), lambda i, l: (l[0], 0, 0)),
            pl.BlockSpec((1, D_MODEL), lambda i, l: (0, 0)),
        ],
        out_specs=[
            pl.BlockSpec((tm, D_MODEL), lambda i, l: (i, 0)),
            pl.BlockSpec((1, CONV_W - 1, 2 * D_FF), lambda i, l: (i // tiles_per_batch, 0, 0)),
        ],
        scratch_shapes=[pltpu.VMEM((tm, D_MODEL), BF16), pltpu.VMEM((tm, D_FF), BF16),
                        pltpu.VMEM((CONV_W - 1, 2 * D_FF), F32)])
    body = functools.partial(_ffn_kernel, tiles_per_batch=tiles_per_batch)
    return pl.pallas_call(body, grid_spec=grid_spec,
                          out_shape=(jax.ShapeDtypeStruct(x.shape, F32),
                                     jax.ShapeDtypeStruct((t // seq, CONV_W - 1, 2 * D_FF), F32)),
                          compiler_params=_cparams(1), name="ffn")(
                              layer, x, gpre, w_up, conv_w, conv_b, w_down, gpost)


def _ffn_sample_kernel(l_ref, x_ref, gpre_ref, wup_ref, cw_ref, cb_ref, wdn_ref, gpost_ref, buf_ref,
                       o_ref, tail_ref, act_ref):
    x = x_ref[...]
    h = _rms(x, gpre_ref[...]).astype(BF16)

    def conv(u, cols):
        u2 = buf_ref[:, 0, cols]
        u1 = buf_ref[:, 1, cols]
        tail_ref[:, 0, cols] = u1
        tail_ref[:, 1, cols] = u
        return cb_ref[:, cols] + u2 * cw_ref[0:1, cols] + u1 * cw_ref[1:2, cols] + u * cw_ref[2:3, cols]

    for j in range(D_FF // FC):
        ca_cols = slice(j * FC, (j + 1) * FC)
        cb_cols = slice(D_FF + j * FC, D_FF + (j + 1) * FC)
        ca = conv(_dot(h, wup_ref[:, ca_cols]), ca_cols)
        cb = conv(_dot(h, wup_ref[:, cb_cols]), cb_cols)
        act_ref[:, ca_cols] = (ca * _sigmoid(ca) * cb).astype(BF16)
    o_ref[...] = x + _rms(_dot(act_ref[...], wdn_ref[...]), gpost_ref[...])


def _ffn_sample(layer, x, gpre, w_up, conv_w, conv_b, w_down, gpost, buf):
    n = x.shape[0]
    grid_spec = pltpu.PrefetchScalarGridSpec(
        num_scalar_prefetch=1, grid=(1,),
        in_specs=[
            pl.BlockSpec((n, D_MODEL), lambda i, l: (0, 0)),
            pl.BlockSpec((1, D_MODEL), lambda i, l: (0, 0)),
            _resident((None, D_MODEL, 2 * D_FF), lambda i, l: (l[0], 0, 0)),
            pl.BlockSpec((CONV_W, 2 * D_FF), lambda i, l: (0, 0)),
            pl.BlockSpec((1, 2 * D_FF), lambda i, l: (0, 0)),
            _resident((None, D_FF, D_MODEL), lambda i, l: (l[0], 0, 0)),
            pl.BlockSpec((1, D_MODEL), lambda i, l: (0, 0)),
            pl.BlockSpec((None, n, CONV_W - 1, 2 * D_FF), lambda i, l: (l[0], 0, 0, 0)),
        ],
        out_specs=[pl.BlockSpec((n, D_MODEL), lambda i, l: (0, 0)),
                   pl.BlockSpec((n, CONV_W - 1, 2 * D_FF), lambda i, l: (0, 0, 0))],
        scratch_shapes=[pltpu.VMEM((n, D_FF), BF16)])
    return pl.pallas_call(_ffn_sample_kernel, grid_spec=grid_spec,
                          out_shape=(jax.ShapeDtypeStruct(x.shape, F32),
                                     jax.ShapeDtypeStruct((n, CONV_W - 1, 2 * D_FF), F32)),
                          compiler_params=_cparams(1), name="ffn_sample")(
                              layer, x, gpre, w_up, conv_w, conv_b, w_down, gpost, buf)


def _s_attn_kernel(l_ref, q_ref, kv_ref, ckt_ref, cvt_ref, sink_ref, o_ref):
    bt = q_ref.shape[0]
    inst = [(bi, g) for bi in range(bt) for g in range(N_KV)]
    hs = [slice(g * GQA_G, (g + 1) * GQA_G) for g in range(N_KV)]
    qs = [q_ref[bi, hs[g], :] for bi, g in inst]
    s = [_dot(q.astype(BF16), ckt_ref[bi, g].astype(BF16)) for q, (bi, g) in zip(qs, inst)]
    s_new = [jnp.sum(q * kv_ref[bi:bi + 1, g * HEAD_DIM:(g + 1) * HEAD_DIM], axis=-1, keepdims=True)
             for q, (bi, g) in zip(qs, inst)]
    sk = [sink_ref[hs[g], :] for g in range(N_KV)]
    m = [jnp.maximum(jnp.maximum(jnp.max(a, axis=-1, keepdims=True), b), sk[g])
         for a, b, (_, g) in zip(s, s_new, inst)]
    p = [jnp.exp(a - mm) for a, mm in zip(s, m)]
    p_new = [jnp.exp(b - mm) for b, mm in zip(s_new, m)]
    den = [jnp.sum(pp, axis=-1, keepdims=True) + pn + jnp.exp(sk[g] - mm)
           for pp, pn, mm, (_, g) in zip(p, p_new, m, inst)]
    for pp, pn, dd, (bi, g) in zip(p, p_new, den, inst):
        v_new = kv_ref[bi:bi + 1, KV_W + g * HEAD_DIM:KV_W + (g + 1) * HEAD_DIM]
        o = _dot_nt(pp.astype(BF16), cvt_ref[bi, g].astype(BF16)) + pn * v_new
        o_ref[bi, hs[g], :] = o / dd


def _s_attn(layer, q, kv, cache_kt, cache_vt, sink_col):
    n = q.shape[0]
    w = cache_kt.shape[-1]
    grid_spec = pltpu.PrefetchScalarGridSpec(
        num_scalar_prefetch=1, grid=(n // S_BT,),
        in_specs=[
            pl.BlockSpec((S_BT, N_HEADS, HEAD_DIM), lambda i, l: (i, 0, 0)),
            pl.BlockSpec((S_BT, 2 * KV_W), lambda i, l: (i, 0)),
            pl.BlockSpec((None, S_BT, N_KV, HEAD_DIM, w), lambda i, l: (l[0], i, 0, 0, 0)),
            pl.BlockSpec((None, S_BT, N_KV, HEAD_DIM, w), lambda i, l: (l[0], i, 0, 0, 0)),
            pl.BlockSpec((N_HEADS, 1), lambda i, l: (0, 0)),
        ],
        out_specs=pl.BlockSpec((S_BT, N_HEADS, HEAD_DIM), lambda i, l: (i, 0, 0)))
    return pl.pallas_call(_s_attn_kernel, grid_spec=grid_spec,
                          out_shape=jax.ShapeDtypeStruct((n, N_HEADS, HEAD_DIM), F32),
                          compiler_params=_cparams(1), name="s_attn")(
                              layer, q.reshape(n, N_HEADS, HEAD_DIM), kv, cache_kt, cache_vt, sink_col
                          ).reshape(n, ATTN_W)


def _s_hgrn_kernel(l_ref, r_ref, rt_ref, st_ref, lbt_ref, gn_ref, alias_ref, o_ref, so_ref):
    del alias_ref
    bt = r_ref.shape[0]
    for h in range(R_HEADS):
        sl = slice(h * R_DK, (h + 1) * R_DK)
        zt = rt_ref[R_KW + h * R_DK:R_KW + (h + 1) * R_DK, :]
        lbc = lbt_ref[sl, :]
        logf, kt = _forget_and_key(zt, lbc)
        f = jnp.exp(logf)
        qr = r_ref[:, sl]
        q16 = (qr * _sigmoid(qr)).astype(BF16)
        vs = slice(2 * R_KW + h * R_DV, 2 * R_KW + (h + 1) * R_DV)
        gs = slice(2 * R_KW + R_VW + h * R_DV, 2 * R_KW + R_VW + (h + 1) * R_DV)
        s_new = [f[:, bi:bi + 1] * st_ref[bi, h] + kt[:, bi:bi + 1] * r_ref[bi:bi + 1, vs]
                 for bi in range(bt)]
        for bi in range(bt):
            so_ref[bi, h] = s_new[bi]
        o = jnp.concatenate([_dot(q16[bi:bi + 1, :], s_new[bi].astype(BF16))
                             for bi in range(bt)], axis=0)
        on = o * lax.rsqrt(jnp.mean(o * o, axis=-1, keepdims=True) + EPS) * gn_ref[:, sl]
        gr = r_ref[:, gs]
        o_ref[:, sl] = (on * (gr * _sigmoid(gr))).astype(o_ref.dtype)


def _s_hgrn(layer, r, state, lb_col, gn, new_states):
    n = r.shape[0]
    rt = r.reshape(n // S_BT, S_BT, R_IN_W).transpose(0, 2, 1)
    state_spec = pl.BlockSpec((None, S_BT, R_HEADS, R_DK, R_DV), lambda i, l: (l[0], i, 0, 0, 0))
    grid_spec = pltpu.PrefetchScalarGridSpec(
        num_scalar_prefetch=1, grid=(n // S_BT,),
        in_specs=[
            pl.BlockSpec((S_BT, R_IN_W), lambda i, l: (i, 0)),
            pl.BlockSpec((None, R_IN_W, S_BT), lambda i, l: (i, 0, 0)),
            state_spec,
            pl.BlockSpec((R_KW, 1), lambda i, l: (0, 0)),
            pl.BlockSpec((1, R_VW), lambda i, l: (0, 0)),
            pl.BlockSpec(memory_space=pl.ANY),
        ],
        out_specs=[pl.BlockSpec((S_BT, R_VW), lambda i, l: (i, 0)), state_spec])
    return pl.pallas_call(_s_hgrn_kernel, grid_spec=grid_spec,
                          out_shape=(jax.ShapeDtypeStruct((n, R_VW), F32),
                                     jax.ShapeDtypeStruct(new_states.shape, F32)),
                          input_output_aliases={6: 1},
                          compiler_params=_cparams(1), name="s_hgrn")(
                              layer, r, rt, state, lb_col, gn, new_states)


def _s_xattn_kernel(l_ref, x_ref, gpre_ref, wq_ref, ck_ref, cv_ref, wo_ref, gpost_ref, o_ref):
    bt, n_mem = ck_ref.shape[0], ck_ref.shape[1]
    n_dt = X_HEAD_DIM // LANES
    ones = jnp.ones((LANES, LANES), BF16)
    x = x_ref[...]
    h = _rms(x, gpre_ref[...]).astype(BF16)
    q = _dot(h, wq_ref[...]) * (X_HEAD_DIM ** -0.5)
    rows = []
    for bi in range(bt):
        qrep = jnp.concatenate(
            [q[bi:bi + 1, hd * X_HEAD_DIM + dt * LANES:hd * X_HEAD_DIM + (dt + 1) * LANES]
             for dt in range(n_dt) for hd in range(X_HEADS)], axis=0)
        prod = (ck_ref[bi] * qrep[None]).reshape(n_mem * X_ROWS, LANES).astype(BF16)
        part = _dot(prod, ones).reshape(n_mem, X_ROWS, LANES)
        s = part + pltpu.roll(part, X_HEADS, 1)
        p = jnp.exp(s - jnp.max(s, axis=0, keepdims=True))
        o = jnp.sum(p * cv_ref[bi], axis=0) / jnp.sum(p, axis=0)
        rows.append(jnp.concatenate([o[dt * X_HEADS + hd:dt * X_HEADS + hd + 1, :]
                                     for hd in range(X_HEADS) for dt in range(n_dt)], axis=1))
    y = _dot(jnp.concatenate(rows, axis=0).astype(BF16), wo_ref[...])
    o_ref[...] = x + _rms(y, gpost_ref[...])


def _s_xattn(layer, x, gpre, w_xq, cache_k, cache_v, w_xo, gpost):
    n = x.shape[0]
    grid_spec = pltpu.PrefetchScalarGridSpec(
        num_scalar_prefetch=1, grid=(n // X_BT,),
        in_specs=[
            pl.BlockSpec((X_BT, D_MODEL), lambda i, l: (i, 0)),
            pl.BlockSpec((1, D_MODEL), lambda i, l: (0, 0)),
            _resident((None, D_MODEL, D_MODEL), lambda i, l: (l[0], 0, 0)),
            pl.BlockSpec((None, X_BT, N_MEM, X_ROWS, LANES), lambda i, l: (l[0], i, 0, 0, 0)),
            pl.BlockSpec((None, X_BT, N_MEM, X_ROWS, LANES), lambda i, l: (l[0], i, 0, 0, 0)),
            _resident((None, D_MODEL, D_MODEL), lambda i, l: (l[0], 0, 0)),
            pl.BlockSpec((1, D_MODEL), lambda i, l: (0, 0)),
        ],
        out_specs=pl.BlockSpec((X_BT, D_MODEL), lambda i, l: (i, 0)))
    return pl.pallas_call(_s_xattn_kernel, grid_spec=grid_spec,
                          out_shape=jax.ShapeDtypeStruct(x.shape, F32),
                          compiler_params=_cparams(1), name="s_xattn")(
                              layer, x, gpre, w_xq, cache_k, cache_v, w_xo, gpost)


def kernel(x_prompt, x_sample, cache_win_k, cache_win_v, cache_mem_k, cache_mem_v, state_hgrn, cache_ffn_conv, mem_prompt, w_in, w_o, attn_sinks, lb_logits, hgrn_norm, w_xq, w_xk, w_xv, w_xo, w_up, conv_w, conv_b, w_down, g_pre_mix, g_post_mix, g_pre_x, g_post_x, g_mem, g_pre_ffn, g_post_ffn):
    batch, seq, _ = x_prompt.shape
    n_s, dec_seq, _ = x_sample.shape
    depth = w_in.shape[0]
    assert dec_seq == 1 and seq % WINDOW == 0 and seq % HC == 0
    assert cache_win_k.shape[2] == WINDOW and n_s % S_BT == 0 and n_s % X_BT == 0
    tm = min(TM, seq)
    assert seq % tm == 0

    w_in_b, w_o_b, w_xq_b, w_xk_b, w_xv_b, w_xo_b, w_up_b, w_down_b = (
        w.astype(BF16) for w in (w_in, w_o, w_xq, w_xk, w_xv, w_xo, w_up, w_down))
    lb, cos_p, sin_p, cos_s, sin_s = _prep(lb_logits, seq, n_s)
    mk5, mv5, mk, mv = _memkv(mem_prompt.reshape(batch * N_MEM, D_MODEL), g_mem.reshape(depth, 1, D_MODEL),
                              w_xk_b, w_xv_b)

    cwk = jnp.transpose(cache_win_k, (0, 1, 3, 4, 2))
    cwv = jnp.transpose(cache_win_v, (0, 1, 3, 4, 2))
    def tile_view(c):
        c = c.reshape(depth, n_s, N_MEM, X_HEADS, X_HEAD_DIM // LANES, LANES)
        return jnp.transpose(c, (0, 1, 2, 4, 3, 5)).reshape(depth, n_s, N_MEM, X_ROWS, LANES)

    cmk, cmv = tile_view(cache_mem_k), tile_view(cache_mem_v)

    def layer_fn(carry, l):
        xp, xs, st_s = carry
        layer = jnp.reshape(l, (1,)).astype(jnp.int32)

        def vec(a):
            return lax.dynamic_index_in_dim(a, l, 0, keepdims=True)

        sink, lb_l, gn_l = vec(attn_sinks), vec(lb), vec(hgrn_norm)
        cw_l, cb_l = lax.dynamic_index_in_dim(conv_w, l, 0, keepdims=False), vec(conv_b)
        gpm, gom, gpx, gox, gpf, gof = (vec(g) for g in (g_pre_mix, g_post_mix, g_pre_x, g_post_x,
                                                        g_pre_ffn, g_post_ffn))

        q_p, kv_p, r_p = _proj_in(layer, xp, gpm, w_in_b, cos_p, sin_p, tm, BF16)
        attn_p = _swa(q_p, kv_p, sink, seq)
        rec_p, st_p = _hgrn(r_p, lb_l, gn_l, batch, seq)
        xp = _mix_xattn(layer, attn_p, rec_p, w_o_b, xp, gom, gpx, w_xq_b, mk, mv, w_xo_b, gox, seq, tm)

        q_s, kv_s, r_s = _proj_in(layer, xs, gpm, w_in_b, cos_s, sin_s, n_s, F32)
        attn_s = _s_attn(layer, q_s, kv_s, cwk, cwv, sink.reshape(N_HEADS, 1))
        rec_s, st_s = _s_hgrn(layer, r_s, state_hgrn, lb_l.reshape(R_KW, 1), gn_l, st_s)
        xs = _mix_out(layer, attn_s, rec_s, w_o_b, xs, gom, n_s)
        xs = _s_xattn(layer, xs, gpx, w_xq_b, cmk, cmv, w_xo_b, gox)

        xp, tail_p = _ffn(layer, xp, gpf, w_up_b, cw_l, cb_l, w_down_b, gof, seq, tm)
        xs, tail_s = _ffn_sample(layer, xs, gpf, w_up_b, cw_l, cb_l, w_down_b, gof, cache_ffn_conv)

        kv_keep = kv_p.reshape(batch, seq, 2 * KV_W)[:, seq - WINDOW:, :]
        ys = (kv_keep[..., :KV_W], kv_keep[..., KV_W:], kv_s[:, :KV_W], kv_s[:, KV_W:],
              st_p, tail_p, tail_s)
        return (xp, xs, st_s), ys

    init = (x_prompt.reshape(batch * seq, D_MODEL), x_sample.reshape(n_s, D_MODEL),
            jnp.zeros(state_hgrn.shape, F32))
    (xp, xs, st_s), ys = lax.scan(layer_fn, init, jnp.arange(depth, dtype=jnp.int32))
    wk_p, wv_p, wk_s, wv_s, st_p, tail_p, tail_s = ys
    return (xp.reshape(batch, seq, D_MODEL), xs.reshape(n_s, 1, D_MODEL),
            wk_p.reshape(depth, batch, WINDOW, N_KV, HEAD_DIM), wv_p.reshape(depth, batch, WINDOW, N_KV, HEAD_DIM),
            wk_s.reshape(depth, n_s, 1, N_KV, HEAD_DIM), wv_s.reshape(depth, n_s, 1, N_KV, HEAD_DIM),
            mk5.reshape(depth, batch, N_MEM, X_HEADS, X_HEAD_DIM), mv5.reshape(depth, batch, N_MEM, X_HEADS, X_HEAD_DIM),
            st_p, st_s,
            tail_p, tail_s)
```

```python
import functools
import math

import jax
import jax.numpy as jnp
from jax import lax
from jax.experimental import pallas as pl
from jax.experimental.pallas import tpu as pltpu

F32 = jnp.float32
BF16 = jnp.bfloat16

D_MODEL = 1024
HEAD_DIM = 64
N_HEADS = 8
N_KV = 2
GQA_G = N_HEADS // N_KV
WINDOW = 128
ROPE_THETA = 10000.0
R_HEADS = 4
R_DK = 128
R_DV = 128
ATTN_W = N_HEADS * HEAD_DIM
KV_W = N_KV * HEAD_DIM
R_KW = R_HEADS * R_DK
R_VW = R_HEADS * R_DV
R_IN_W = 2 * R_KW + 2 * R_VW
P_IN = ATTN_W + 2 * KV_W + R_IN_W
N_MEM = 256
X_HEADS = 4
X_HEAD_DIM = D_MODEL // X_HEADS
X_ROWS = X_HEADS * (X_HEAD_DIM // 128)
D_FF = 2816
CONV_W = 3
EPS = 1e-6
PAST_LEN = 8192

LANES = 128
SUBLANES = 8
VMEM_LIMIT = 56 * 1024 * 1024

TM = 512
SWA_TQ = 512
HC = 128
HG_ROWS = 512
LOG2_E = 1.4426950408889634
FC = 256
S_BT = 8
X_BT = 8


def _cparams(n_axes):
    return pltpu.CompilerParams(dimension_semantics=("arbitrary",) * n_axes,
                                vmem_limit_bytes=VMEM_LIMIT)


def _rms(x, g):
    return x * lax.rsqrt(jnp.mean(x * x, axis=-1, keepdims=True) + EPS) * g


def _sigmoid(x):
    return 1.0 / (1.0 + jnp.exp(-x))


def _dot(a, b):
    return jnp.dot(a, b, preferred_element_type=F32)


def _dot_nt(a, b):
    return lax.dot_general(a, b, (((1,), (1,)), ((), ())), preferred_element_type=F32)


def _dot_tn(a, b):
    return lax.dot_general(a, b, (((0,), (0,)), ((), ())), preferred_element_type=F32)


def _resident(block_shape, index_map):
    return pl.BlockSpec(block_shape, index_map, pipeline_mode=pl.Buffered(1))


def _prep_kernel(lbl_ref, lb_ref, cos_ref, sin_ref, cos_s_ref, sin_s_ref):
    x = lbl_ref[...]
    e = jnp.exp(x - jnp.max(x, axis=0, keepdims=True))
    sm = e / jnp.sum(e, axis=0, keepdims=True)
    depth = x.shape[0]
    acc = jnp.zeros((1, x.shape[1]), F32)
    rows = [acc]
    for l in range(1, depth):
        acc = acc + sm[l:l + 1, :]
        rows.append(acc)
    lb_ref[...] = jnp.concatenate(rows, axis=0)

    half = HEAD_DIM // 2

    def tables(shape, pos):
        lane = lax.broadcasted_iota(jnp.int32, shape, 1)
        j = (lane & (half - 1)).astype(F32)
        inv_freq = jnp.exp(j * (-math.log(ROPE_THETA) / half))
        ang = pos * inv_freq
        first = (lane & (HEAD_DIM - 1)) < half
        return jnp.cos(ang), jnp.where(first, -jnp.sin(ang), jnp.sin(ang))

    pos_p = lax.broadcasted_iota(jnp.int32, cos_ref.shape, 0).astype(F32)
    c, s = tables(cos_ref.shape, pos_p)
    cos_ref[...] = c
    sin_ref[...] = s
    c, s = tables(cos_s_ref.shape, jnp.full(cos_s_ref.shape, float(PAST_LEN), F32))
    cos_s_ref[...] = c
    sin_s_ref[...] = s


def _prep(lb_logits, seq, n_sample):
    depth = lb_logits.shape[0]
    out_shape = (jax.ShapeDtypeStruct((depth, R_KW), F32),
                 jax.ShapeDtypeStruct((seq, LANES), F32), jax.ShapeDtypeStruct((seq, LANES), F32),
                 jax.ShapeDtypeStruct((n_sample, LANES), F32), jax.ShapeDtypeStruct((n_sample, LANES), F32))
    return pl.pallas_call(_prep_kernel, out_shape=out_shape, name="prep")(lb_logits)


def _rope(x, cos, sin):
    w = x.shape[-1]
    reps = w // LANES
    if reps > 1:
        cos = jnp.tile(cos, (1, reps))
        sin = jnp.tile(sin, (1, reps))
    half = HEAD_DIM // 2
    lane = lax.broadcasted_iota(jnp.int32, x.shape, 1)
    first = (lane & (HEAD_DIM - 1)) < half
    swapped = jnp.where(first, pltpu.roll(x, w - half, 1), pltpu.roll(x, half, 1))
    return x * cos + swapped * sin


def _proj_in_kernel(l_ref, x_ref, g_ref, w_ref, cos_ref, sin_ref, q_ref, kv_ref, r_ref):
    h = _rms(x_ref[...], g_ref[...]).astype(BF16)
    cos = cos_ref[...]
    sin = sin_ref[...]
    q = _dot(h, w_ref[:, 0:ATTN_W])
    q_ref[...] = (_rope(q, cos, sin) * (HEAD_DIM ** -0.5)).astype(q_ref.dtype)
    k = _dot(h, w_ref[:, ATTN_W:ATTN_W + KV_W])
    kv_ref[:, 0:KV_W] = _rope(k, cos, sin)
    kv_ref[:, KV_W:2 * KV_W] = _dot(h, w_ref[:, ATTN_W + KV_W:ATTN_W + 2 * KV_W])
    r_ref[...] = _dot(h, w_ref[:, ATTN_W + 2 * KV_W:P_IN])


def _proj_in(layer, x, g, w_in, cos, sin, tm, q_dtype):
    t = x.shape[0]
    n_tab = cos.shape[0] // tm
    grid_spec = pltpu.PrefetchScalarGridSpec(
        num_scalar_prefetch=1, grid=(t // tm,),
        in_specs=[
            pl.BlockSpec((tm, D_MODEL), lambda i, l: (i, 0)),
            pl.BlockSpec((1, D_MODEL), lambda i, l: (0, 0)),
            _resident((None, D_MODEL, P_IN), lambda i, l: (l[0], 0, 0)),
            pl.BlockSpec((tm, LANES), lambda i, l: (i % n_tab, 0)),
            pl.BlockSpec((tm, LANES), lambda i, l: (i % n_tab, 0)),
        ],
        out_specs=[
            pl.BlockSpec((tm, ATTN_W), lambda i, l: (i, 0)),
            pl.BlockSpec((tm, 2 * KV_W), lambda i, l: (i, 0)),
            pl.BlockSpec((tm, R_IN_W), lambda i, l: (i, 0)),
        ])
    out_shape = (jax.ShapeDtypeStruct((t, ATTN_W), q_dtype),
                 jax.ShapeDtypeStruct((t, 2 * KV_W), F32),
                 jax.ShapeDtypeStruct((t, R_IN_W), F32))
    return pl.pallas_call(_proj_in_kernel, grid_spec=grid_spec, out_shape=out_shape,
                          compiler_params=_cparams(1), name="proj_in")(layer, x, g, w_in, cos, sin)


def _swa_kernel(q_ref, kv_ref, kvp_ref, sink_ref, o_ref, *, tiles_per_batch):
    i = pl.program_id(0)
    w = WINDOW
    nblk = q_ref.shape[0] // w
    first_key = jnp.where(i % tiles_per_batch == 0, w, 0)
    row = lax.broadcasted_iota(jnp.int32, (w, 2 * w), 0)
    col = lax.broadcasted_iota(jnp.int32, (w, 2 * w), 1)
    band = (col >= row) & (col <= row + w)
    low_o = lax.broadcasted_iota(jnp.int32, (w, LANES), 1) < HEAD_DIM
    low_v = lax.broadcasted_iota(jnp.int32, (2 * w, LANES), 1) < HEAD_DIM
    zeros_half = jnp.zeros((HEAD_DIM, 2 * w), BF16)
    valid, k_pad, vcat = [], [], []
    for b in range(nblk):
        if b == 0:
            kv2 = jnp.concatenate([kvp_ref[...], kv_ref[0:w, :]], axis=0)
            valid.append(band & (col >= first_key))
        else:
            kv2 = kv_ref[(b - 1) * w:(b + 1) * w, :]
            valid.append(band)
        kt = kv2[:, 0:KV_W].T.astype(BF16)
        v2 = kv2[:, KV_W:2 * KV_W]
        v_sw = pltpu.roll(v2, HEAD_DIM, 1)
        for g in range(N_KV):
            kg = kt[g * HEAD_DIM:(g + 1) * HEAD_DIM, :]
            k_pad.append((jnp.concatenate([kg, zeros_half], axis=0), jnp.concatenate([zeros_half, kg], axis=0)))
            if g == 0:
                v_lo, v_hi = jnp.where(low_v, v2, 0.0), jnp.where(low_v, 0.0, v_sw)
            else:
                v_lo, v_hi = jnp.where(low_v, v_sw, 0.0), jnp.where(low_v, 0.0, v2)
            vcat.append(jnp.concatenate([v_lo, v_hi], axis=0).astype(BF16))

    inst = [(b, h) for b in range(nblk) for h in range(N_HEADS)]
    scores = [jnp.where(valid[b],
                        _dot(q_ref[b * w:(b + 1) * w, (h // 2) * LANES:(h // 2 + 1) * LANES],
                             k_pad[b * N_KV + h // GQA_G][h % 2]),
                        -jnp.inf) for b, h in inst]
    sinks = [sink_ref[0:1, h:h + 1] for _, h in inst]
    maxes = [jnp.maximum(jnp.max(s, axis=-1, keepdims=True), sk) for s, sk in zip(scores, sinks)]
    probs = [jnp.exp(s - m) for s, m in zip(scores, maxes)]
    rden = [1.0 / (jnp.sum(p, axis=-1, keepdims=True) + jnp.exp(sk - m))
            for p, sk, m in zip(probs, sinks, maxes)]
    for b in range(nblk):
        for pr in range(N_HEADS // 2):
            lo = b * N_HEADS + 2 * pr
            ps = slice(pr * LANES, (pr + 1) * LANES)
            pcat = jnp.concatenate([probs[lo].astype(BF16), probs[lo + 1].astype(BF16)], axis=1)
            o = _dot(pcat, vcat[b * N_KV + (2 * pr) // GQA_G])
            o_ref[b * w:(b + 1) * w, ps] = (o * jnp.where(low_o, rden[lo], rden[lo + 1])).astype(o_ref.dtype)


def _swa(q, kv, sink, seq):
    t = q.shape[0]
    tq = min(SWA_TQ, seq)
    nblk = tq // WINDOW
    body = functools.partial(_swa_kernel, tiles_per_batch=seq // tq)
    return pl.pallas_call(
        body, grid=(t // tq,),
        in_specs=[
            pl.BlockSpec((tq, ATTN_W), lambda i: (i, 0)),
            pl.BlockSpec((tq, 2 * KV_W), lambda i: (i, 0)),
            pl.BlockSpec((WINDOW, 2 * KV_W), lambda i: (jnp.maximum(i * nblk - 1, 0), 0)),
            pl.BlockSpec((1, N_HEADS), lambda i: (0, 0)),
        ],
        out_specs=pl.BlockSpec((tq, ATTN_W), lambda i: (i, 0)),
        out_shape=jax.ShapeDtypeStruct(q.shape, BF16),
        compiler_params=_cparams(1), name="swa")(q, kv, kv, sink)


def _forget_and_key(z, lb):
    e = jnp.exp(-jnp.abs(z))
    logsig = jnp.minimum(z, 0.0) - jnp.log(1.0 + e)
    a = jnp.log(lb)
    c = jnp.log(1.0 - lb) + logsig
    logf = jnp.maximum(a, c) + jnp.log(1.0 + jnp.exp(-jnp.abs(a - c)))
    key = (1.0 - lb) * (jnp.where(z >= 0.0, e, 1.0) / (1.0 + e))
    return logf, key


def _cumsum_rows(g, tril):
    hi = g.astype(BF16)
    r1 = g - hi.astype(F32)
    mid = r1.astype(BF16)
    lo = (r1 - mid.astype(F32)).astype(BF16)
    return _dot(tril, hi) + _dot(tril, mid) + _dot(tril, lo)


def _level_ref(b, half):
    n_rows, width = b.shape
    n = 2 * half
    if n >= 2 * SUBLANES:
        pieces = [jnp.broadcast_to(b[i * n + half - 1:i * n + half, :], (n, width))
                  for i in range(n_rows // n)]
        return pieces[0] if len(pieces) == 1 else jnp.concatenate(pieces, axis=0)
    b3 = b.reshape(n_rows // SUBLANES, SUBLANES, width)
    sub = lax.broadcasted_iota(jnp.int32, b3.shape, 1)

    def bcast(r):
        return jnp.broadcast_to(b3[:, r:r + 1, :], b3.shape)

    if half == 4:
        ref = bcast(3)
    elif half == 2:
        ref = jnp.where(sub < 4, bcast(1), bcast(5))
    else:
        ref = jnp.where(sub < 2, bcast(0), jnp.where(sub < 4, bcast(2), jnp.where(sub < 6, bcast(4), bcast(6))))
    return ref.reshape(n_rows, width)


def _hgrn_kernel(qr_ref, fr_ref, ir_ref, gr_ref, lb_ref, gn_ref, o_ref, s_out_ref, st_ref):
    c = pl.program_id(1)
    last = pl.num_programs(1) - 1

    @pl.when(c == 0)
    def _():
        st_ref[...] = jnp.zeros_like(st_ref)

    rr = lax.broadcasted_iota(jnp.int32, (HC, HC), 0)
    cc = lax.broadcasted_iota(jnp.int32, (HC, HC), 1)
    xor = rr ^ cc
    causal = cc <= rr
    tril = jnp.where(causal, 1.0, 0.0).astype(BF16)
    halves = [1 << i for i in range(HC.bit_length() - 1)]
    row_w = lax.broadcasted_iota(jnp.int32, (HC, R_KW), 0)
    signs = [jnp.where((row_w & half) != 0, 1.0, -1.0) for half in halves]
    heads = [slice(h * R_DK, (h + 1) * R_DK) for h in range(R_HEADS)]
    st = [st_ref[h] for h in range(R_HEADS)]

    for ci in range(qr_ref.shape[0] // HC):
        rows = slice(ci * HC, (ci + 1) * HC)
        g, k = _forget_and_key(fr_ref[rows, :], lb_ref[...])
        qr = qr_ref[rows, :]
        q = qr * _sigmoid(qr)
        v16 = ir_ref[rows, :].astype(BF16)
        b = _cumsum_rows(g * LOG2_E, tril)
        q16 = q.astype(BF16)
        k16 = k.astype(BF16)
        b_last = b[HC - 1:HC, :]
        q_in = (q * jnp.exp2(b)).astype(BF16)
        k_out = (k * jnp.exp2(b_last - b)).astype(BF16)
        decay = jnp.exp2(b_last)
        gr = gr_ref[rows, :]
        gate = gr * _sigmoid(gr)
        q_lv, k_lv = [], []
        for half, sgn in zip(halves, signs):
            e = jnp.exp2((b - _level_ref(b, half)) * sgn).astype(BF16)
            q_lv.append(q16 * e)
            k_lv.append(k16 * e)

        diag = [_dot_nt(q16[:, sl], k16[:, sl]) for sl in heads]
        levels = [[_dot_nt(ql[:, sl], kl[:, sl]) for sl in heads] for ql, kl in zip(q_lv, k_lv)]
        att16 = []
        for h in range(R_HEADS):
            att = diag[h]
            for half, lv in zip(halves, levels):
                att = jnp.where(xor >= half, lv[h], att)
            att16.append(jnp.where(causal, att, 0.0).astype(BF16))
        outs = [_dot(att16[h], v16[:, sl]) + _dot_nt(q_in[:, sl], st[h].astype(BF16))
                for h, sl in enumerate(heads)]
        st = [st[h] * decay[:, sl] + _dot_tn(v16[:, sl], k_out[:, sl]) for h, sl in enumerate(heads)]
        for h, sl in enumerate(heads):
            o = outs[h]
            on = o * lax.rsqrt(jnp.mean(o * o, axis=-1, keepdims=True) + EPS) * gn_ref[:, sl]
            o_ref[rows, sl] = (on * gate[:, sl]).astype(o_ref.dtype)

    for h in range(R_HEADS):
        st_ref[h] = st[h]

    @pl.when(c == last)
    def _():
        for h in range(R_HEADS):
            s_out_ref[0, h] = st[h].T


def _hgrn(r, lb, gn, batch, seq):
    rows = min(HG_ROWS, seq)
    nc = seq // rows
    t = r.shape[0]

    def col(kk):
        return pl.BlockSpec((rows, R_KW), lambda b, c: (b * nc + c, kk))

    return pl.pallas_call(
        _hgrn_kernel, grid=(batch, nc),
        in_specs=[col(0), col(1), col(2), col(3),
                  pl.BlockSpec((1, R_KW), lambda b, c: (0, 0)),
                  pl.BlockSpec((1, R_VW), lambda b, c: (0, 0))],
        out_specs=[pl.BlockSpec((rows, R_VW), lambda b, c: (b * nc + c, 0)),
                   pl.BlockSpec((1, R_HEADS, R_DK, R_DV), lambda b, c: (b, 0, 0, 0))],
        out_shape=(jax.ShapeDtypeStruct((t, R_VW), BF16),
                   jax.ShapeDtypeStruct((batch, R_HEADS, R_DK, R_DV), F32)),
        scratch_shapes=[pltpu.VMEM((R_HEADS, R_DV, R_DK), F32)],
        compiler_params=_cparams(2), name="hgrn")(r, r, r, r, lb, gn)


def _mix_out_kernel(l_ref, a_ref, r_ref, w_ref, x_ref, g_ref, o_ref):
    m = (_dot(a_ref[...].astype(BF16), w_ref[0:ATTN_W, :])
         + _dot(r_ref[...].astype(BF16), w_ref[ATTN_W:ATTN_W + R_VW, :]))
    o_ref[...] = x_ref[...] + _rms(m, g_ref[...])


def _mix_out(layer, attn, rec, w_o, x, g, tm):
    t = x.shape[0]
    grid_spec = pltpu.PrefetchScalarGridSpec(
        num_scalar_prefetch=1, grid=(t // tm,),
        in_specs=[
            pl.BlockSpec((tm, ATTN_W), lambda i, l: (i, 0)),
            pl.BlockSpec((tm, R_VW), lambda i, l: (i, 0)),
            _resident((None, ATTN_W + R_VW, D_MODEL), lambda i, l: (l[0], 0, 0)),
            pl.BlockSpec((tm, D_MODEL), lambda i, l: (i, 0)),
            pl.BlockSpec((1, D_MODEL), lambda i, l: (0, 0)),
        ],
        out_specs=pl.BlockSpec((tm, D_MODEL), lambda i, l: (i, 0)))
    return pl.pallas_call(_mix_out_kernel, grid_spec=grid_spec,
                          out_shape=jax.ShapeDtypeStruct(x.shape, F32),
                          compiler_params=_cparams(1), name="mix_out")(layer, attn, rec, w_o, x, g)


def _memkv_kernel(m_ref, g_ref, wk_ref, wv_ref, k_ref, v_ref, k16_ref, v16_ref):
    h = _rms(m_ref[...], g_ref[...]).astype(BF16)
    for w_ref, o_ref, o16_ref in ((wk_ref, k_ref, k16_ref), (wv_ref, v_ref, v16_ref)):
        y = _dot(h, w_ref[...])
        o16_ref[...] = y.astype(BF16)
        for hd in range(X_HEADS):
            o_ref[:, hd, :] = y[:, hd * X_HEAD_DIM:(hd + 1) * X_HEAD_DIM]


def _memkv(mem, g_mem, w_xk, w_xv):
    depth = w_xk.shape[0]
    rows = mem.shape[0]
    tm = min(TM, rows)
    out = jax.ShapeDtypeStruct((depth, rows, X_HEADS, X_HEAD_DIM), F32)
    out16 = jax.ShapeDtypeStruct((depth, rows, D_MODEL), BF16)
    return pl.pallas_call(
        _memkv_kernel, grid=(depth, rows // tm),
        in_specs=[
            pl.BlockSpec((tm, D_MODEL), lambda l, i: (i, 0)),
            pl.BlockSpec((None, 1, D_MODEL), lambda l, i: (l, 0, 0)),
            pl.BlockSpec((None, D_MODEL, D_MODEL), lambda l, i: (l, 0, 0)),
            pl.BlockSpec((None, D_MODEL, D_MODEL), lambda l, i: (l, 0, 0)),
        ],
        out_specs=[pl.BlockSpec((None, tm, X_HEADS, X_HEAD_DIM), lambda l, i: (l, i, 0, 0)),
                   pl.BlockSpec((None, tm, X_HEADS, X_HEAD_DIM), lambda l, i: (l, i, 0, 0)),
                   pl.BlockSpec((None, tm, D_MODEL), lambda l, i: (l, i, 0)),
                   pl.BlockSpec((None, tm, D_MODEL), lambda l, i: (l, i, 0))],
        out_shape=(out, out, out16, out16), compiler_params=_cparams(2), name="memkv")(mem, g_mem, w_xk, w_xv)


def _mix_xattn_kernel(l_ref, a_ref, r_ref, wmix_ref, x_ref, gmix_ref,
                      gpre_ref, wq_ref, mk_ref, mv_ref, wo_ref, gpost_ref, o_ref):
    m = (_dot(a_ref[...], wmix_ref[0:ATTN_W, :]) + _dot(r_ref[...], wmix_ref[ATTN_W:ATTN_W + R_VW, :]))
    x = x_ref[...] + _rms(m, gmix_ref[...])
    h = _rms(x, gpre_ref[...]).astype(BF16)
    q = (_dot(h, wq_ref[...]) * (X_HEAD_DIM ** -0.5)).astype(BF16)
    heads = [slice(hd * X_HEAD_DIM, (hd + 1) * X_HEAD_DIM) for hd in range(X_HEADS)]
    s = [_dot_nt(q[:, sl], mk_ref[:, sl]) for sl in heads]
    p = [jnp.exp(a - jnp.max(a, axis=-1, keepdims=True)) for a in s]
    den = [jnp.sum(a, axis=-1, keepdims=True) for a in p]
    outs = [(_dot(a.astype(BF16), mv_ref[:, sl]) / d).astype(BF16) for a, d, sl in zip(p, den, heads)]
    y = _dot(jnp.concatenate(outs, axis=1), wo_ref[...])
    o_ref[...] = x + _rms(y, gpost_ref[...])


def _mix_xattn(layer, attn, rec, w_o, x, gmix, gpre, w_xq, mk, mv, w_xo, gpost, seq, tm):
    t = x.shape[0]
    tpb = seq // tm
    grid_spec = pltpu.PrefetchScalarGridSpec(
        num_scalar_prefetch=1, grid=(t // tm,),
        in_specs=[
            pl.BlockSpec((tm, ATTN_W), lambda i, l: (i, 0)),
            pl.BlockSpec((tm, R_VW), lambda i, l: (i, 0)),
            _resident((None, ATTN_W + R_VW, D_MODEL), lambda i, l: (l[0], 0, 0)),
            pl.BlockSpec((tm, D_MODEL), lambda i, l: (i, 0)),
            pl.BlockSpec((1, D_MODEL), lambda i, l: (0, 0)),
            pl.BlockSpec((1, D_MODEL), lambda i, l: (0, 0)),
            _resident((None, D_MODEL, D_MODEL), lambda i, l: (l[0], 0, 0)),
            pl.BlockSpec((None, N_MEM, D_MODEL), lambda i, l: (l[0], i // tpb, 0)),
            pl.BlockSpec((None, N_MEM, D_MODEL), lambda i, l: (l[0], i // tpb, 0)),
            _resident((None, D_MODEL, D_MODEL), lambda i, l: (l[0], 0, 0)),
            pl.BlockSpec((1, D_MODEL), lambda i, l: (0, 0)),
        ],
        out_specs=pl.BlockSpec((tm, D_MODEL), lambda i, l: (i, 0)))
    return pl.pallas_call(_mix_xattn_kernel, grid_spec=grid_spec,
                          out_shape=jax.ShapeDtypeStruct(x.shape, F32),
                          compiler_params=_cparams(1), name="mix_xattn")(
                              layer, attn, rec, w_o, x, gmix, gpre, w_xq, mk, mv, w_xo, gpost)


def _conv_chunk(u, cols, cw_ref, cb_ref, carry_ref, sub0):
    tm = u.shape[0]
    w0 = cw_ref[0:1, cols]
    w1 = cw_ref[1:2, cols]
    w2 = cw_ref[2:3, cols]

    def shift1(a, first):
        rolled = pltpu.roll(a, 1, 0)
        head = jnp.where(sub0, first, rolled[0:SUBLANES])
        return jnp.concatenate([head, rolled[SUBLANES:]], axis=0)

    p0 = carry_ref[0:1, cols]
    p1 = carry_ref[1:2, cols]
    carry_ref[:, cols] = u[tm - (CONV_W - 1):tm, :]
    v = u * w1 + shift1(u * w0, p1 * w0)
    return cb_ref[:, cols] + u * w2 + shift1(v, p1 * w1 + p0 * w0)


def _ffn_kernel(l_ref, x_ref, gpre_ref, wup_ref, cw_ref, cb_ref, wdn_ref, gpost_ref,
                o_ref, tail_ref, h_ref, act_ref, carry_ref, *, tiles_per_batch):
    i = pl.program_id(0)

    @pl.when(i % tiles_per_batch == 0)
    def _():
        carry_ref[...] = jnp.zeros_like(carry_ref)

    h_ref[...] = _rms(x_ref[...], gpre_ref[...]).astype(BF16)
    sub0 = lax.broadcasted_iota(jnp.int32, (SUBLANES, FC), 0) == 0
    for j in range(D_FF // FC):
        ca_cols = slice(j * FC, (j + 1) * FC)
        cb_cols = slice(D_FF + j * FC, D_FF + (j + 1) * FC)
        ca = _conv_chunk(_dot(h_ref[...], wup_ref[:, ca_cols]), ca_cols, cw_ref, cb_ref, carry_ref, sub0)
        cb = _conv_chunk(_dot(h_ref[...], wup_ref[:, cb_cols]), cb_cols, cw_ref, cb_ref, carry_ref, sub0)
        act_ref[:, ca_cols] = (ca * _sigmoid(ca) * cb).astype(BF16)
    o_ref[...] = x_ref[...] + _rms(_dot(act_ref[...], wdn_ref[...]), gpost_ref[...])

    @pl.when(i % tiles_per_batch == tiles_per_batch - 1)
    def _():
        tail_ref[0] = carry_ref[...]


def _ffn(layer, x, gpre, w_up, conv_w, conv_b, w_down, gpost, seq, tm):
    t = x.shape[0]
    tiles_per_batch = seq // tm
    grid_spec = pltpu.PrefetchScalarGridSpec(
        num_scalar_prefetch=1, grid=(t // tm,),
        in_specs=[
            pl.BlockSpec((tm, D_MODEL), lambda i, l: (i, 0)),
            pl.BlockSpec((1, D_MODEL), lambda i, l: (0, 0)),
            _resident((None, D_MODEL, 2 * D_FF), lambda i, l: (l[0], 0, 0)),
            pl.BlockSpec((CONV_W, 2 * D_FF), lambda i, l: (0, 0)),
            pl.BlockSpec((1, 2 * D_FF), lambda i, l: (0, 0)),
            _resident((None, D_FF, D_MODEL), lambda i, l: (l[0], 0, 0)),
            pl.BlockSpec((1, D_MODEL), lambda i, l: (0, 0)),
        ],
        out_specs=[
            pl.BlockSpec((tm, D_MODEL), lambda i, l: (i, 0)),
            pl.BlockSpec((1, CONV_W - 1, 2 * D_FF), lambda i, l: (i // tiles_per_batch, 0, 0)),
        ],
        scratch_shapes=[pltpu.VMEM((tm, D_MODEL), BF16), pltpu.VMEM((tm, D_FF), BF16),
                        pltpu.VMEM((CONV_W - 1, 2 * D_FF), F32)])
    body = functools.partial(_ffn_kernel, tiles_per_batch=tiles_per_batch)
    return pl.pallas_call(body, grid_spec=grid_spec,
                          out_shape=(jax.ShapeDtypeStruct(x.shape, F32),
                                     jax.ShapeDtypeStruct((t // seq, CONV_W - 1, 2 * D_FF), F32)),
                          compiler_params=_cparams(1), name="ffn")(
                              layer, x, gpre, w_up, conv_w, conv_b, w_down, gpost)


def _ffn_sample_kernel(l_ref, x_ref, gpre_ref, wup_ref, cw_ref, cb_ref, wdn_ref, gpost_ref, buf_ref,
                       o_ref, tail_ref, act_ref):
    x = x_ref[...]
    h = _rms(x, gpre_ref[...]).astype(BF16)

    def conv(u, cols):
        u2 = buf_ref[:, 0, cols]
        u1 = buf_ref[:, 1, cols]
        tail_ref[:, 0, cols] = u1
        tail_ref[:, 1, cols] = u
        return cb_ref[:, cols] + u2 * cw_ref[0:1, cols] + u1 * cw_ref[1:2, cols] + u * cw_ref[2:3, cols]

    for j in range(D_FF // FC):
        ca_cols = slice(j * FC, (j + 1) * FC)
        cb_cols = slice(D_FF + j * FC, D_FF + (j + 1) * FC)
        ca = conv(_dot(h, wup_ref[:, ca_cols]), ca_cols)
        cb = conv(_dot(h, wup_ref[:, cb_cols]), cb_cols)
        act_ref[:, ca_cols] = (ca * _sigmoid(ca) * cb).astype(BF16)
    o_ref[...] = x + _rms(_dot(act_ref[...], wdn_ref[...]), gpost_ref[...])


def _ffn_sample(layer, x, gpre, w_up, conv_w, conv_b, w_down, gpost, buf):
    n = x.shape[0]
    grid_spec = pltpu.PrefetchScalarGridSpec(
        num_scalar_prefetch=1, grid=(1,),
        in_specs=[
            pl.BlockSpec((n, D_MODEL), lambda i, l: (0, 0)),
            pl.BlockSpec((1, D_MODEL), lambda i, l: (0, 0)),
            _resident((None, D_MODEL, 2 * D_FF), lambda i, l: (l[0], 0, 0)),
            pl.BlockSpec((CONV_W, 2 * D_FF), lambda i, l: (0, 0)),
            pl.BlockSpec((1, 2 * D_FF), lambda i, l: (0, 0)),
            _resident((None, D_FF, D_MODEL), lambda i, l: (l[0], 0, 0)),
            pl.BlockSpec((1, D_MODEL), lambda i, l: (0, 0)),
            pl.BlockSpec((None, n, CONV_W - 1, 2 * D_FF), lambda i, l: (l[0], 0, 0, 0)),
        ],
        out_specs=[pl.BlockSpec((n, D_MODEL), lambda i, l: (0, 0)),
                   pl.BlockSpec((n, CONV_W - 1, 2 * D_FF), lambda i, l: (0, 0, 0))],
        scratch_shapes=[pltpu.VMEM((n, D_FF), BF16)])
    return pl.pallas_call(_ffn_sample_kernel, grid_spec=grid_spec,
                          out_shape=(jax.ShapeDtypeStruct(x.shape, F32),
                                     jax.ShapeDtypeStruct((n, CONV_W - 1, 2 * D_FF), F32)),
                          compiler_params=_cparams(1), name="ffn_sample")(
                              layer, x, gpre, w_up, conv_w, conv_b, w_down, gpost, buf)


def _s_attn_kernel(l_ref, q_ref, kv_ref, ckt_ref, cvt_ref, sink_ref, o_ref):
    bt = q_ref.shape[0]
    inst = [(bi, g) for bi in range(bt) for g in range(N_KV)]
    hs = [slice(g * GQA_G, (g + 1) * GQA_G) for g in range(N_KV)]
    qs = [q_ref[bi, hs[g], :] for bi, g in inst]
    s = [_dot(q.astype(BF16), ckt_ref[bi, g].astype(BF16)) for q, (bi, g) in zip(qs, inst)]
    s_new = [jnp.sum(q * kv_ref[bi:bi + 1, g * HEAD_DIM:(g + 1) * HEAD_DIM], axis=-1, keepdims=True)
             for q, (bi, g) in zip(qs, inst)]
    sk = [sink_ref[hs[g], :] for g in range(N_KV)]
    m = [jnp.maximum(jnp.maximum(jnp.max(a, axis=-1, keepdims=True), b), sk[g])
         for a, b, (_, g) in zip(s, s_new, inst)]
    p = [jnp.exp(a - mm) for a, mm in zip(s, m)]
    p_new = [jnp.exp(b - mm) for b, mm in zip(s_new, m)]
    den = [jnp.sum(pp, axis=-1, keepdims=True) + pn + jnp.exp(sk[g] - mm)
           for pp, pn, mm, (_, g) in zip(p, p_new, m, inst)]
    for pp, pn, dd, (bi, g) in zip(p, p_new, den, inst):
        v_new = kv_ref[bi:bi + 1, KV_W + g * HEAD_DIM:KV_W + (g + 1) * HEAD_DIM]
        o = _dot_nt(pp.astype(BF16), cvt_ref[bi, g].astype(BF16)) + pn * v_new
        o_ref[bi, hs[g], :] = o / dd


def _s_attn(layer, q, kv, cache_kt, cache_vt, sink_col):
    n = q.shape[0]
    w = cache_kt.shape[-1]
    grid_spec = pltpu.PrefetchScalarGridSpec(
        num_scalar_prefetch=1, grid=(n // S_BT,),
        in_specs=[
            pl.BlockSpec((S_BT, N_HEADS, HEAD_DIM), lambda i, l: (i, 0, 0)),
            pl.BlockSpec((S_BT, 2 * KV_W), lambda i, l: (i, 0)),
            pl.BlockSpec((None, S_BT, N_KV, HEAD_DIM, w), lambda i, l: (l[0], i, 0, 0, 0)),
            pl.BlockSpec((None, S_BT, N_KV, HEAD_DIM, w), lambda i, l: (l[0], i, 0, 0, 0)),
            pl.BlockSpec((N_HEADS, 1), lambda i, l: (0, 0)),
        ],
        out_specs=pl.BlockSpec((S_BT, N_HEADS, HEAD_DIM), lambda i, l: (i, 0, 0)))
    return pl.pallas_call(_s_attn_kernel, grid_spec=grid_spec,
                          out_shape=jax.ShapeDtypeStruct((n, N_HEADS, HEAD_DIM), F32),
                          compiler_params=_cparams(1), name="s_attn")(
                              layer, q.reshape(n, N_HEADS, HEAD_DIM), kv, cache_kt, cache_vt, sink_col
                          ).reshape(n, ATTN_W)


def _s_hgrn_kernel(l_ref, r_ref, rt_ref, st_ref, lbt_ref, gn_ref, alias_ref, o_ref, so_ref):
    del alias_ref
    bt = r_ref.shape[0]
    for h in range(R_HEADS):
        sl = slice(h * R_DK, (h + 1) * R_DK)
        zt = rt_ref[R_KW + h * R_DK:R_KW + (h + 1) * R_DK, :]
        lbc = lbt_ref[sl, :]
        logf, kt = _forget_and_key(zt, lbc)
        f = jnp.exp(logf)
        qr = r_ref[:, sl]
        q16 = (qr * _sigmoid(qr)).astype(BF16)
        vs = slice(2 * R_KW + h * R_DV, 2 * R_KW + (h + 1) * R_DV)
        gs = slice(2 * R_KW + R_VW + h * R_DV, 2 * R_KW + R_VW + (h + 1) * R_DV)
        s_new = [f[:, bi:bi + 1] * st_ref[bi, h] + kt[:, bi:bi + 1] * r_ref[bi:bi + 1, vs]
                 for bi in range(bt)]
        for bi in range(bt):
            so_ref[bi, h] = s_new[bi]
        o = jnp.concatenate([_dot(q16[bi:bi + 1, :], s_new[bi].astype(BF16))
                             for bi in range(bt)], axis=0)
        on = o * lax.rsqrt(jnp.mean(o * o, axis=-1, keepdims=True) + EPS) * gn_ref[:, sl]
        gr = r_ref[:, gs]
        o_ref[:, sl] = (on * (gr * _sigmoid(gr))).astype(o_ref.dtype)


def _s_hgrn(layer, r, state, lb_col, gn, new_states):
    n = r.shape[0]
    rt = r.reshape(n // S_BT, S_BT, R_IN_W).transpose(0, 2, 1)
    state_spec = pl.BlockSpec((None, S_BT, R_HEADS, R_DK, R_DV), lambda i, l: (l[0], i, 0, 0, 0))
    grid_spec = pltpu.PrefetchScalarGridSpec(
        num_scalar_prefetch=1, grid=(n // S_BT,),
        in_specs=[
            pl.BlockSpec((S_BT, R_IN_W), lambda i, l: (i, 0)),
            pl.BlockSpec((None, R_IN_W, S_BT), lambda i, l: (i, 0, 0)),
            state_spec,
            pl.BlockSpec((R_KW, 1), lambda i, l: (0, 0)),
            pl.BlockSpec((1, R_VW), lambda i, l: (0, 0)),
            pl.BlockSpec(memory_space=pl.ANY),
        ],
        out_specs=[pl.BlockSpec((S_BT, R_VW), lambda i, l: (i, 0)), state_spec])
    return pl.pallas_call(_s_hgrn_kernel, grid_spec=grid_spec,
                          out_shape=(jax.ShapeDtypeStruct((n, R_VW), F32),
                                     jax.ShapeDtypeStruct(new_states.shape, F32)),
                          input_output_aliases={6: 1},
                          compiler_params=_cparams(1), name="s_hgrn")(
                              layer, r, rt, state, lb_col, gn, new_states)


def _s_xattn_kernel(l_ref, x_ref, gpre_ref, wq_ref, ck_ref, cv_ref, wo_ref, gpost_ref, o_ref):
    bt, n_mem = ck_ref.shape[0], ck_ref.shape[1]
    n_dt = X_HEAD_DIM // LANES
    ones = jnp.ones((LANES, LANES), BF16)
    x = x_ref[...]
    h = _rms(x, gpre_ref[...]).astype(BF16)
    q = _dot(h, wq_ref[...]) * (X_HEAD_DIM ** -0.5)
    rows = []
    for bi in range(bt):
        qrep = jnp.concatenate(
            [q[bi:bi + 1, hd * X_HEAD_DIM + dt * LANES:hd * X_HEAD_DIM + (dt + 1) * LANES]
             for dt in range(n_dt) for hd in range(X_HEADS)], axis=0)
        prod = (ck_ref[bi] * qrep[None]).reshape(n_mem * X_ROWS, LANES).astype(BF16)
        part = _dot(prod, ones).reshape(n_mem, X_ROWS, LANES)
        s = part + pltpu.roll(part, X_HEADS, 1)
        p = jnp.exp(s - jnp.max(s, axis=0, keepdims=True))
        o = jnp.sum(p * cv_ref[bi], axis=0) / jnp.sum(p, axis=0)
        rows.append(jnp.concatenate([o[dt * X_HEADS + hd:dt * X_HEADS + hd + 1, :]
                                     for hd in range(X_HEADS) for dt in range(n_dt)], axis=1))
    y = _dot(jnp.concatenate(rows, axis=0).astype(BF16), wo_ref[...])
    o_ref[...] = x + _rms(y, gpost_ref[...])


def _s_xattn(layer, x, gpre, w_xq, cache_k, cache_v, w_xo, gpost):
    n = x.shape[0]
    grid_spec = pltpu.PrefetchScalarGridSpec(
        num_scalar_prefetch=1, grid=(n // X_BT,),
        in_specs=[
            pl.BlockSpec((X_BT, D_MODEL), lambda i, l: (i, 0)),
            pl.BlockSpec((1, D_MODEL), lambda i, l: (0, 0)),
            _resident((None, D_MODEL, D_MODEL), lambda i, l: (l[0], 0, 0)),
            pl.BlockSpec((None, X_BT, N_MEM, X_ROWS, LANES), lambda i, l: (l[0], i, 0, 0, 0)),
            pl.BlockSpec((None, X_BT, N_MEM, X_ROWS, LANES), lambda i, l: (l[0], i, 0, 0, 0)),
            _resident((None, D_MODEL, D_MODEL), lambda i, l: (l[0], 0, 0)),
            pl.BlockSpec((1, D_MODEL), lambda i, l: (0, 0)),
        ],
        out_specs=pl.BlockSpec((X_BT, D_MODEL), lambda i, l: (i, 0)))
    return pl.pallas_call(_s_xattn_kernel, grid_spec=grid_spec,
                          out_shape=jax.ShapeDtypeStruct(x.shape, F32),
                          compiler_params=_cparams(1), name="s_xattn")(
                              layer, x, gpre, w_xq, cache_k, cache_v, w_xo, gpost)


def kernel(x_prompt, x_sample, cache_win_k, cache_win_v, cache_mem_k, cache_mem_v, state_hgrn, cache_ffn_conv, mem_prompt, w_in, w_o, attn_sinks, lb_logits, hgrn_norm, w_xq, w_xk, w_xv, w_xo, w_up, conv_w, conv_b, w_down, g_pre_mix, g_post_mix, g_pre_x, g_post_x, g_mem, g_pre_ffn, g_post_ffn):
    batch, seq, _ = x_prompt.shape
    n_s, dec_seq, _ = x_sample.shape
    depth = w_in.shape[0]
    assert dec_seq == 1 and seq % WINDOW == 0 and seq % HC == 0
    assert cache_win_k.shape[2] == WINDOW and n_s % S_BT == 0 and n_s % X_BT == 0
    tm = min(TM, seq)
    assert seq % tm == 0

    w_in_b, w_o_b, w_xq_b, w_xk_b, w_xv_b, w_xo_b, w_up_b, w_down_b = (
        w.astype(BF16) for w in (w_in, w_o, w_xq, w_xk, w_xv, w_xo, w_up, w_down))
    lb, cos_p, sin_p, cos_s, sin_s = _prep(lb_logits, seq, n_s)
    mk5, mv5, mk, mv = _memkv(mem_prompt.reshape(batch * N_MEM, D_MODEL), g_mem.reshape(depth, 1, D_MODEL),
                              w_xk_b, w_xv_b)

    cwk = jnp.transpose(cache_win_k, (0, 1, 3, 4, 2))
    cwv = jnp.transpose(cache_win_v, (0, 1, 3, 4, 2))
    def tile_view(c):
        c = c.reshape(depth, n_s, N_MEM, X_HEADS, X_HEAD_DIM // LANES, LANES)
        return jnp.transpose(c, (0, 1, 2, 4, 3, 5)).reshape(depth, n_s, N_MEM, X_ROWS, LANES)

    cmk, cmv = tile_view(cache_mem_k), tile_view(cache_mem_v)

    xp = x_prompt.reshape(batch * seq, D_MODEL)
    xs = x_sample.reshape(n_s, D_MODEL)
    st_s = jnp.zeros(state_hgrn.shape, F32)
    per_layer = []
    for l in range(depth):
        layer = jnp.full((1,), l, jnp.int32)
        sink, lb_l, gn_l, cb_l = (a[l:l + 1] for a in (attn_sinks, lb, hgrn_norm, conv_b))
        cw_l = conv_w[l]
        gpm, gom, gpx, gox, gpf, gof = (g[l:l + 1] for g in (g_pre_mix, g_post_mix, g_pre_x, g_post_x,
                                                            g_pre_ffn, g_post_ffn))

        q_p, kv_p, r_p = _proj_in(layer, xp, gpm, w_in_b, cos_p, sin_p, tm, BF16)
        attn_p = _swa(q_p, kv_p, sink, seq)
        rec_p, st_p = _hgrn(r_p, lb_l, gn_l, batch, seq)
        xp = _mix_xattn(layer, attn_p, rec_p, w_o_b, xp, gom, gpx, w_xq_b, mk, mv, w_xo_b, gox, seq, tm)

        q_s, kv_s, r_s = _proj_in(layer, xs, gpm, w_in_b, cos_s, sin_s, n_s, F32)
        attn_s = _s_attn(layer, q_s, kv_s, cwk, cwv, sink.reshape(N_HEADS, 1))
        rec_s, st_s = _s_hgrn(layer, r_s, state_hgrn, lb_l.reshape(R_KW, 1), gn_l, st_s)
        xs = _mix_out(layer, attn_s, rec_s, w_o_b, xs, gom, n_s)
        xs = _s_xattn(layer, xs, gpx, w_xq_b, cmk, cmv, w_xo_b, gox)

        xp, tail_p = _ffn(layer, xp, gpf, w_up_b, cw_l, cb_l, w_down_b, gof, seq, tm)
        xs, tail_s = _ffn_sample(layer, xs, gpf, w_up_b, cw_l, cb_l, w_down_b, gof, cache_ffn_conv)

        kv_keep = kv_p.reshape(batch, seq, 2 * KV_W)[:, seq - WINDOW:, :]
        per_layer.append((kv_keep[..., :KV_W], kv_keep[..., KV_W:], kv_s[:, :KV_W], kv_s[:, KV_W:],
                          st_p, tail_p, tail_s))

    wk_p, wv_p, wk_s, wv_s, st_p, tail_p, tail_s = (jnp.stack(a) for a in zip(*per_layer))
    return (xp.reshape(batch, seq, D_MODEL), xs.reshape(n_s, 1, D_MODEL),
            wk_p.reshape(depth, batch, WINDOW, N_KV, HEAD_DIM), wv_p.reshape(depth, batch, WINDOW, N_KV, HEAD_DIM),
            wk_s.reshape(depth, n_s, 1, N_KV, HEAD_DIM), wv_s.reshape(depth, n_s, 1, N_KV, HEAD_DIM),
            mk5.reshape(depth, batch, N_MEM, X_HEADS, X_HEAD_DIM), mv5.reshape(depth, batch, N_MEM, X_HEADS, X_HEAD_DIM),
            st_p, st_s,
            tail_p, tail_s)
```

```python
import functools
import math

import jax
import jax.numpy as jnp
from jax import lax
from jax.experimental import pallas as pl
from jax.experimental.pallas import tpu as pltpu

F32 = jnp.float32
BF16 = jnp.bfloat16

D_MODEL = 1024
HEAD_DIM = 64
N_HEADS = 8
N_KV = 2
GQA_G = N_HEADS // N_KV
WINDOW = 128
ROPE_THETA = 10000.0
R_HEADS = 4
R_DK = 128
R_DV = 128
ATTN_W = N_HEADS * HEAD_DIM
KV_W = N_KV * HEAD_DIM
R_KW = R_HEADS * R_DK
R_VW = R_HEADS * R_DV
R_IN_W = 2 * R_KW + 2 * R_VW
P_IN = ATTN_W + 2 * KV_W + R_IN_W
N_MEM = 256
X_HEADS = 4
X_HEAD_DIM = D_MODEL // X_HEADS
X_ROWS = X_HEADS * (X_HEAD_DIM // 128)
D_FF = 2816
CONV_W = 3
EPS = 1e-6
PAST_LEN = 8192

LANES = 128
SUBLANES = 8
VMEM_LIMIT = 56 * 1024 * 1024

TM = 1024
SWA_TQ = 1024
HC = 128
HG_ROWS = 1024
LOG2_E = 1.4426950408889634
FC = 256
S_BT = 8
X_BT = 8


def _cparams(n_axes):
    return pltpu.CompilerParams(dimension_semantics=("arbitrary",) * n_axes,
                                vmem_limit_bytes=VMEM_LIMIT)


def _rms(x, g):
    return x * lax.rsqrt(jnp.mean(x * x, axis=-1, keepdims=True) + EPS) * g


def _sigmoid(x):
    return 1.0 / (1.0 + jnp.exp(-x))


def _dot(a, b):
    return jnp.dot(a, b, preferred_element_type=F32)


def _dot_nt(a, b):
    return lax.dot_general(a, b, (((1,), (1,)), ((), ())), preferred_element_type=F32)


def _dot_tn(a, b):
    return lax.dot_general(a, b, (((0,), (0,)), ((), ())), preferred_element_type=F32)


def _resident(block_shape, index_map):
    return pl.BlockSpec(block_shape, index_map, pipeline_mode=pl.Buffered(1))


def _prep_kernel(lbl_ref, lb_ref, cos_ref, sin_ref, cos_s_ref, sin_s_ref):
    x = lbl_ref[...]
    e = jnp.exp(x - jnp.max(x, axis=0, keepdims=True))
    sm = e / jnp.sum(e, axis=0, keepdims=True)
    depth = x.shape[0]
    acc = jnp.zeros((1, x.shape[1]), F32)
    rows = [acc]
    for l in range(1, depth):
        acc = acc + sm[l:l + 1, :]
        rows.append(acc)
    lb_ref[...] = jnp.concatenate(rows, axis=0)

    half = HEAD_DIM // 2

    def tables(shape, pos):
        lane = lax.broadcasted_iota(jnp.int32, shape, 1)
        j = (lane & (half - 1)).astype(F32)
        inv_freq = jnp.exp(j * (-math.log(ROPE_THETA) / half))
        ang = pos * inv_freq
        first = (lane & (HEAD_DIM - 1)) < half
        return jnp.cos(ang), jnp.where(first, -jnp.sin(ang), jnp.sin(ang))

    pos_p = lax.broadcasted_iota(jnp.int32, cos_ref.shape, 0).astype(F32)
    c, s = tables(cos_ref.shape, pos_p)
    cos_ref[...] = c
    sin_ref[...] = s
    c, s = tables(cos_s_ref.shape, jnp.full(cos_s_ref.shape, float(PAST_LEN), F32))
    cos_s_ref[...] = c
    sin_s_ref[...] = s


def _prep(lb_logits, seq, n_sample):
    depth = lb_logits.shape[0]
    out_shape = (jax.ShapeDtypeStruct((depth, R_KW), F32),
                 jax.ShapeDtypeStruct((seq, LANES), F32), jax.ShapeDtypeStruct((seq, LANES), F32),
                 jax.ShapeDtypeStruct((n_sample, LANES), F32), jax.ShapeDtypeStruct((n_sample, LANES), F32))
    return pl.pallas_call(_prep_kernel, out_shape=out_shape, name="prep")(lb_logits)


def _rope(x, cos, sin):
    w = x.shape[-1]
    reps = w // LANES
    if reps > 1:
        cos = jnp.tile(cos, (1, reps))
        sin = jnp.tile(sin, (1, reps))
    half = HEAD_DIM // 2
    lane = lax.broadcasted_iota(jnp.int32, x.shape, 1)
    first = (lane & (HEAD_DIM - 1)) < half
    swapped = jnp.where(first, pltpu.roll(x, w - half, 1), pltpu.roll(x, half, 1))
    return x * cos + swapped * sin


def _proj_in_kernel(l_ref, x_ref, g_ref, w_ref, cos_ref, sin_ref, q_ref, kv_ref, r_ref):
    h = _rms(x_ref[...], g_ref[...]).astype(BF16)
    cos = cos_ref[...]
    sin = sin_ref[...]
    q = _dot(h, w_ref[:, 0:ATTN_W])
    q_ref[...] = (_rope(q, cos, sin) * (HEAD_DIM ** -0.5)).astype(q_ref.dtype)
    k = _dot(h, w_ref[:, ATTN_W:ATTN_W + KV_W])
    kv_ref[:, 0:KV_W] = _rope(k, cos, sin)
    kv_ref[:, KV_W:2 * KV_W] = _dot(h, w_ref[:, ATTN_W + KV_W:ATTN_W + 2 * KV_W])
    r_ref[...] = _dot(h, w_ref[:, ATTN_W + 2 * KV_W:P_IN])


def _proj_in(layer, x, g, w_in, cos, sin, tm, q_dtype):
    t = x.shape[0]
    n_tab = cos.shape[0] // tm
    grid_spec = pltpu.PrefetchScalarGridSpec(
        num_scalar_prefetch=1, grid=(t // tm,),
        in_specs=[
            pl.BlockSpec((tm, D_MODEL), lambda i, l: (i, 0)),
            pl.BlockSpec((1, D_MODEL), lambda i, l: (0, 0)),
            _resident((None, D_MODEL, P_IN), lambda i, l: (l[0], 0, 0)),
            pl.BlockSpec((tm, LANES), lambda i, l: (i % n_tab, 0)),
            pl.BlockSpec((tm, LANES), lambda i, l: (i % n_tab, 0)),
        ],
        out_specs=[
            pl.BlockSpec((tm, ATTN_W), lambda i, l: (i, 0)),
            pl.BlockSpec((tm, 2 * KV_W), lambda i, l: (i, 0)),
            pl.BlockSpec((tm, R_IN_W), lambda i, l: (i, 0)),
        ])
    out_shape = (jax.ShapeDtypeStruct((t, ATTN_W), q_dtype),
                 jax.ShapeDtypeStruct((t, 2 * KV_W), F32),
                 jax.ShapeDtypeStruct((t, R_IN_W), F32))
    return pl.pallas_call(_proj_in_kernel, grid_spec=grid_spec, out_shape=out_shape,
                          compiler_params=_cparams(1), name="proj_in")(layer, x, g, w_in, cos, sin)


def _swa_kernel(q_ref, kv_ref, kvp_ref, sink_ref, o_ref, *, tiles_per_batch):
    i = pl.program_id(0)
    w = WINDOW
    nblk = q_ref.shape[0] // w
    first_key = jnp.where(i % tiles_per_batch == 0, w, 0)
    row = lax.broadcasted_iota(jnp.int32, (w, 2 * w), 0)
    col = lax.broadcasted_iota(jnp.int32, (w, 2 * w), 1)
    band = (col >= row) & (col <= row + w)
    low_o = lax.broadcasted_iota(jnp.int32, (w, LANES), 1) < HEAD_DIM
    low_v = lax.broadcasted_iota(jnp.int32, (2 * w, LANES), 1) < HEAD_DIM
    zeros_half = jnp.zeros((HEAD_DIM, 2 * w), BF16)
    valid, k_pad, vcat = [], [], []
    for b in range(nblk):
        if b == 0:
            kv2 = jnp.concatenate([kvp_ref[...], kv_ref[0:w, :]], axis=0)
            valid.append(band & (col >= first_key))
        else:
            kv2 = kv_ref[(b - 1) * w:(b + 1) * w, :]
            valid.append(band)
        kt = kv2[:, 0:KV_W].T.astype(BF16)
        v2 = kv2[:, KV_W:2 * KV_W]
        v_sw = pltpu.roll(v2, HEAD_DIM, 1)
        for g in range(N_KV):
            kg = kt[g * HEAD_DIM:(g + 1) * HEAD_DIM, :]
            k_pad.append((jnp.concatenate([kg, zeros_half], axis=0), jnp.concatenate([zeros_half, kg], axis=0)))
            if g == 0:
                v_lo, v_hi = jnp.where(low_v, v2, 0.0), jnp.where(low_v, 0.0, v_sw)
            else:
                v_lo, v_hi = jnp.where(low_v, v_sw, 0.0), jnp.where(low_v, 0.0, v2)
            vcat.append(jnp.concatenate([v_lo, v_hi], axis=0).astype(BF16))

    inst = [(b, h) for b in range(nblk) for h in range(N_HEADS)]
    scores = [jnp.where(valid[b],
                        _dot(q_ref[b * w:(b + 1) * w, (h // 2) * LANES:(h // 2 + 1) * LANES],
                             k_pad[b * N_KV + h // GQA_G][h % 2]),
                        -jnp.inf) for b, h in inst]
    sinks = [sink_ref[0:1, h:h + 1] for _, h in inst]
    maxes = [jnp.maximum(jnp.max(s, axis=-1, keepdims=True), sk) for s, sk in zip(scores, sinks)]
    probs = [jnp.exp(s - m) for s, m in zip(scores, maxes)]
    rden = [1.0 / (jnp.sum(p, axis=-1, keepdims=True) + jnp.exp(sk - m))
            for p, sk, m in zip(probs, sinks, maxes)]
    for b in range(nblk):
        for pr in range(N_HEADS // 2):
            lo = b * N_HEADS + 2 * pr
            ps = slice(pr * LANES, (pr + 1) * LANES)
            pcat = jnp.concatenate([probs[lo].astype(BF16), probs[lo + 1].astype(BF16)], axis=1)
            o = _dot(pcat, vcat[b * N_KV + (2 * pr) // GQA_G])
            o_ref[b * w:(b + 1) * w, ps] = (o * jnp.where(low_o, rden[lo], rden[lo + 1])).astype(o_ref.dtype)


def _swa(q, kv, sink, seq):
    t = q.shape[0]
    tq = min(SWA_TQ, seq)
    nblk = tq // WINDOW
    body = functools.partial(_swa_kernel, tiles_per_batch=seq // tq)
    return pl.pallas_call(
        body, grid=(t // tq,),
        in_specs=[
            pl.BlockSpec((tq, ATTN_W), lambda i: (i, 0)),
            pl.BlockSpec((tq, 2 * KV_W), lambda i: (i, 0)),
            pl.BlockSpec((WINDOW, 2 * KV_W), lambda i: (jnp.maximum(i * nblk - 1, 0), 0)),
            pl.BlockSpec((1, N_HEADS), lambda i: (0, 0)),
        ],
        out_specs=pl.BlockSpec((tq, ATTN_W), lambda i: (i, 0)),
        out_shape=jax.ShapeDtypeStruct(q.shape, BF16),
        compiler_params=_cparams(1), name="swa")(q, kv, kv, sink)


def _forget_and_key(z, lb):
    e = jnp.exp(-jnp.abs(z))
    logsig = jnp.minimum(z, 0.0) - jnp.log(1.0 + e)
    a = jnp.log(lb)
    c = jnp.log(1.0 - lb) + logsig
    logf = jnp.maximum(a, c) + jnp.log(1.0 + jnp.exp(-jnp.abs(a - c)))
    key = (1.0 - lb) * (jnp.where(z >= 0.0, e, 1.0) / (1.0 + e))
    return logf, key


def _cumsum_rows(g, tril):
    hi = g.astype(BF16)
    r1 = g - hi.astype(F32)
    mid = r1.astype(BF16)
    lo = (r1 - mid.astype(F32)).astype(BF16)
    return _dot(tril, hi) + _dot(tril, mid) + _dot(tril, lo)


def _level_ref(b, half):
    n_rows, width = b.shape
    n = 2 * half
    if n >= 2 * SUBLANES:
        pieces = [jnp.broadcast_to(b[i * n + half - 1:i * n + half, :], (n, width))
                  for i in range(n_rows // n)]
        return pieces[0] if len(pieces) == 1 else jnp.concatenate(pieces, axis=0)
    b3 = b.reshape(n_rows // SUBLANES, SUBLANES, width)
    sub = lax.broadcasted_iota(jnp.int32, b3.shape, 1)

    def bcast(r):
        return jnp.broadcast_to(b3[:, r:r + 1, :], b3.shape)

    if half == 4:
        ref = bcast(3)
    elif half == 2:
        ref = jnp.where(sub < 4, bcast(1), bcast(5))
    else:
        ref = jnp.where(sub < 2, bcast(0), jnp.where(sub < 4, bcast(2), jnp.where(sub < 6, bcast(4), bcast(6))))
    return ref.reshape(n_rows, width)


def _hgrn_kernel(qr_ref, fr_ref, ir_ref, gr_ref, lb_ref, gn_ref, o_ref, s_out_ref, st_ref):
    c = pl.program_id(1)
    last = pl.num_programs(1) - 1

    @pl.when(c == 0)
    def _():
        st_ref[...] = jnp.zeros_like(st_ref)

    rr = lax.broadcasted_iota(jnp.int32, (HC, HC), 0)
    cc = lax.broadcasted_iota(jnp.int32, (HC, HC), 1)
    xor = rr ^ cc
    causal = cc <= rr
    tril = jnp.where(causal, 1.0, 0.0).astype(BF16)
    halves = [1 << i for i in range(HC.bit_length() - 1)]
    row_w = lax.broadcasted_iota(jnp.int32, (HC, R_KW), 0)
    signs = [jnp.where((row_w & half) != 0, 1.0, -1.0) for half in halves]
    heads = [slice(h * R_DK, (h + 1) * R_DK) for h in range(R_HEADS)]
    st = [st_ref[h] for h in range(R_HEADS)]

    for ci in range(qr_ref.shape[0] // HC):
        rows = slice(ci * HC, (ci + 1) * HC)
        g, k = _forget_and_key(fr_ref[rows, :], lb_ref[...])
        qr = qr_ref[rows, :]
        q = qr * _sigmoid(qr)
        v16 = ir_ref[rows, :].astype(BF16)
        b = _cumsum_rows(g * LOG2_E, tril)
        q16 = q.astype(BF16)
        k16 = k.astype(BF16)
        b_last = b[HC - 1:HC, :]
        q_in = (q * jnp.exp2(b)).astype(BF16)
        k_out = (k * jnp.exp2(b_last - b)).astype(BF16)
        decay = jnp.exp2(b_last)
        gr = gr_ref[rows, :]
        gate = gr * _sigmoid(gr)
        q_lv, k_lv = [], []
        for half, sgn in zip(halves, signs):
            e = jnp.exp2((b - _level_ref(b, half)) * sgn).astype(BF16)
            q_lv.append(q16 * e)
            k_lv.append(k16 * e)

        diag = [_dot_nt(q16[:, sl], k16[:, sl]) for sl in heads]
        levels = [[_dot_nt(ql[:, sl], kl[:, sl]) for sl in heads] for ql, kl in zip(q_lv, k_lv)]
        att16 = []
        for h in range(R_HEADS):
            att = diag[h]
            for half, lv in zip(halves, levels):
                att = jnp.where(xor >= half, lv[h], att)
            att16.append(jnp.where(causal, att, 0.0).astype(BF16))
        outs = [_dot(att16[h], v16[:, sl]) + _dot_nt(q_in[:, sl], st[h].astype(BF16))
                for h, sl in enumerate(heads)]
        st = [st[h] * decay[:, sl] + _dot_tn(v16[:, sl], k_out[:, sl]) for h, sl in enumerate(heads)]
        for h, sl in enumerate(heads):
            o = outs[h]
            on = o * lax.rsqrt(jnp.mean(o * o, axis=-1, keepdims=True) + EPS) * gn_ref[:, sl]
            o_ref[rows, sl] = (on * gate[:, sl]).astype(o_ref.dtype)

    for h in range(R_HEADS):
        st_ref[h] = st[h]

    @pl.when(c == last)
    def _():
        for h in range(R_HEADS):
            s_out_ref[0, h] = st[h].T


def _hgrn(r, lb, gn, batch, seq):
    rows = min(HG_ROWS, seq)
    nc = seq // rows
    t = r.shape[0]

    def col(kk):
        return pl.BlockSpec((rows, R_KW), lambda b, c: (b * nc + c, kk))

    return pl.pallas_call(
        _hgrn_kernel, grid=(batch, nc),
        in_specs=[col(0), col(1), col(2), col(3),
                  pl.BlockSpec((1, R_KW), lambda b, c: (0, 0)),
                  pl.BlockSpec((1, R_VW), lambda b, c: (0, 0))],
        out_specs=[pl.BlockSpec((rows, R_VW), lambda b, c: (b * nc + c, 0)),
                   pl.BlockSpec((1, R_HEADS, R_DK, R_DV), lambda b, c: (b, 0, 0, 0))],
        out_shape=(jax.ShapeDtypeStruct((t, R_VW), BF16),
                   jax.ShapeDtypeStruct((batch, R_HEADS, R_DK, R_DV), F32)),
        scratch_shapes=[pltpu.VMEM((R_HEADS, R_DV, R_DK), F32)],
        compiler_params=_cparams(2), name="hgrn")(r, r, r, r, lb, gn)


def _mix_out_kernel(l_ref, a_ref, r_ref, w_ref, x_ref, g_ref, o_ref):
    m = (_dot(a_ref[...].astype(BF16), w_ref[0:ATTN_W, :])
         + _dot(r_ref[...].astype(BF16), w_ref[ATTN_W:ATTN_W + R_VW, :]))
    o_ref[...] = x_ref[...] + _rms(m, g_ref[...])


def _mix_out(layer, attn, rec, w_o, x, g, tm):
    t = x.shape[0]
    grid_spec = pltpu.PrefetchScalarGridSpec(
        num_scalar_prefetch=1, grid=(t // tm,),
        in_specs=[
            pl.BlockSpec((tm, ATTN_W), lambda i, l: (i, 0)),
            pl.BlockSpec((tm, R_VW), lambda i, l: (i, 0)),
            _resident((None, ATTN_W + R_VW, D_MODEL), lambda i, l: (l[0], 0, 0)),
            pl.BlockSpec((tm, D_MODEL), lambda i, l: (i, 0)),
            pl.BlockSpec((1, D_MODEL), lambda i, l: (0, 0)),
        ],
        out_specs=pl.BlockSpec((tm, D_MODEL), lambda i, l: (i, 0)))
    return pl.pallas_call(_mix_out_kernel, grid_spec=grid_spec,
                          out_shape=jax.ShapeDtypeStruct(x.shape, F32),
                          compiler_params=_cparams(1), name="mix_out")(layer, attn, rec, w_o, x, g)


def _memkv_kernel(m_ref, g_ref, wk_ref, wv_ref, k_ref, v_ref, k16_ref, v16_ref):
    h = _rms(m_ref[...], g_ref[...]).astype(BF16)
    for w_ref, o_ref, o16_ref in ((wk_ref, k_ref, k16_ref), (wv_ref, v_ref, v16_ref)):
        y = _dot(h, w_ref[...])
        o16_ref[...] = y.astype(BF16)
        for hd in range(X_HEADS):
            o_ref[:, hd, :] = y[:, hd * X_HEAD_DIM:(hd + 1) * X_HEAD_DIM]


def _memkv(mem, g_mem, w_xk, w_xv):
    depth = w_xk.shape[0]
    rows = mem.shape[0]
    tm = min(TM, rows)
    out = jax.ShapeDtypeStruct((depth, rows, X_HEADS, X_HEAD_DIM), F32)
    out16 = jax.ShapeDtypeStruct((depth, rows, D_MODEL), BF16)
    return pl.pallas_call(
        _memkv_kernel, grid=(depth, rows // tm),
        in_specs=[
            pl.BlockSpec((tm, D_MODEL), lambda l, i: (i, 0)),
            pl.BlockSpec((None, 1, D_MODEL), lambda l, i: (l, 0, 0)),
            pl.BlockSpec((None, D_MODEL, D_MODEL), lambda l, i: (l, 0, 0)),
            pl.BlockSpec((None, D_MODEL, D_MODEL), lambda l, i: (l, 0, 0)),
        ],
        out_specs=[pl.BlockSpec((None, tm, X_HEADS, X_HEAD_DIM), lambda l, i: (l, i, 0, 0)),
                   pl.BlockSpec((None, tm, X_HEADS, X_HEAD_DIM), lambda l, i: (l, i, 0, 0)),
                   pl.BlockSpec((None, tm, D_MODEL), lambda l, i: (l, i, 0)),
                   pl.BlockSpec((None, tm, D_MODEL), lambda l, i: (l, i, 0))],
        out_shape=(out, out, out16, out16), compiler_params=_cparams(2), name="memkv")(mem, g_mem, w_xk, w_xv)


def _mix_xattn_kernel(l_ref, a_ref, r_ref, wmix_ref, x_ref, gmix_ref,
                      gpre_ref, wq_ref, mk_ref, mv_ref, wo_ref, gpost_ref, o_ref):
    m = (_dot(a_ref[...], wmix_ref[0:ATTN_W, :]) + _dot(r_ref[...], wmix_ref[ATTN_W:ATTN_W + R_VW, :]))
    x = x_ref[...] + _rms(m, gmix_ref[...])
    h = _rms(x, gpre_ref[...]).astype(BF16)
    q = (_dot(h, wq_ref[...]) * (X_HEAD_DIM ** -0.5)).astype(BF16)
    heads = [slice(hd * X_HEAD_DIM, (hd + 1) * X_HEAD_DIM) for hd in range(X_HEADS)]
    s = [_dot_nt(q[:, sl], mk_ref[:, sl]) for sl in heads]
    p = [jnp.exp(a - jnp.max(a, axis=-1, keepdims=True)) for a in s]
    den = [jnp.sum(a, axis=-1, keepdims=True) for a in p]
    outs = [(_dot(a.astype(BF16), mv_ref[:, sl]) / d).astype(BF16) for a, d, sl in zip(p, den, heads)]
    y = _dot(jnp.concatenate(outs, axis=1), wo_ref[...])
    o_ref[...] = x + _rms(y, gpost_ref[...])


def _mix_xattn(layer, attn, rec, w_o, x, gmix, gpre, w_xq, mk, mv, w_xo, gpost, seq, tm):
    t = x.shape[0]
    tpb = seq // tm
    grid_spec = pltpu.PrefetchScalarGridSpec(
        num_scalar_prefetch=1, grid=(t // tm,),
        in_specs=[
            pl.BlockSpec((tm, ATTN_W), lambda i, l: (i, 0)),
            pl.BlockSpec((tm, R_VW), lambda i, l: (i, 0)),
            _resident((None, ATTN_W + R_VW, D_MODEL), lambda i, l: (l[0], 0, 0)),
            pl.BlockSpec((tm, D_MODEL), lambda i, l: (i, 0)),
            pl.BlockSpec((1, D_MODEL), lambda i, l: (0, 0)),
            pl.BlockSpec((1, D_MODEL), lambda i, l: (0, 0)),
            _resident((None, D_MODEL, D_MODEL), lambda i, l: (l[0], 0, 0)),
            pl.BlockSpec((None, N_MEM, D_MODEL), lambda i, l: (l[0], i // tpb, 0)),
            pl.BlockSpec((None, N_MEM, D_MODEL), lambda i, l: (l[0], i // tpb, 0)),
            _resident((None, D_MODEL, D_MODEL), lambda i, l: (l[0], 0, 0)),
            pl.BlockSpec((1, D_MODEL), lambda i, l: (0, 0)),
        ],
        out_specs=pl.BlockSpec((tm, D_MODEL), lambda i, l: (i, 0)))
    return pl.pallas_call(_mix_xattn_kernel, grid_spec=grid_spec,
                          out_shape=jax.ShapeDtypeStruct(x.shape, F32),
                          compiler_params=_cparams(1), name="mix_xattn")(
                              layer, attn, rec, w_o, x, gmix, gpre, w_xq, mk, mv, w_xo, gpost)


def _conv_chunk(u, cols, cw_ref, cb_ref, carry_ref, sub0):
    tm = u.shape[0]
    w0 = cw_ref[0:1, cols]
    w1 = cw_ref[1:2, cols]
    w2 = cw_ref[2:3, cols]

    def shift1(a, first):
        rolled = pltpu.roll(a, 1, 0)
        head = jnp.where(sub0, first, rolled[0:SUBLANES])
        return jnp.concatenate([head, rolled[SUBLANES:]], axis=0)

    p0 = carry_ref[0:1, cols]
    p1 = carry_ref[1:2, cols]
    carry_ref[:, cols] = u[tm - (CONV_W - 1):tm, :]
    v = u * w1 + shift1(u * w0, p1 * w0)
    return cb_ref[:, cols] + u * w2 + shift1(v, p1 * w1 + p0 * w0)


def _ffn_kernel(l_ref, x_ref, gpre_ref, wup_ref, cw_ref, cb_ref, wdn_ref, gpost_ref,
                o_ref, tail_ref, h_ref, act_ref, carry_ref, *, tiles_per_batch):
    i = pl.program_id(0)

    @pl.when(i % tiles_per_batch == 0)
    def _():
        carry_ref[...] = jnp.zeros_like(carry_ref)

    h_ref[...] = _rms(x_ref[...], gpre_ref[...]).astype(BF16)
    sub0 = lax.broadcasted_iota(jnp.int32, (SUBLANES, FC), 0) == 0
    for j in range(D_FF // FC):
        ca_cols = slice(j * FC, (j + 1) * FC)
        cb_cols = slice(D_FF + j * FC, D_FF + (j + 1) * FC)
        ca = _conv_chunk(_dot(h_ref[...], wup_ref[:, ca_cols]), ca_cols, cw_ref, cb_ref, carry_ref, sub0)
        cb = _conv_chunk(_dot(h_ref[...], wup_ref[:, cb_cols]), cb_cols, cw_ref, cb_ref, carry_ref, sub0)
        act_ref[:, ca_cols] = (ca * _sigmoid(ca) * cb).astype(BF16)
    o_ref[...] = x_ref[...] + _rms(_dot(act_ref[...], wdn_ref[...]), gpost_ref[...])

    @pl.when(i % tiles_per_batch == tiles_per_batch - 1)
    def _():
        tail_ref[0] = carry_ref[...]


def _ffn(layer, x, gpre, w_up, conv_w, conv_b, w_down, gpost, seq, tm):
    t = x.shape[0]
    tiles_per_batch = seq // tm
    grid_spec = pltpu.PrefetchScalarGridSpec(
        num_scalar_prefetch=1, grid=(t // tm,),
        in_specs=[
            pl.BlockSpec((tm, D_MODEL), lambda i, l: (i, 0)),
            pl.BlockSpec((1, D_MODEL), lambda i, l: (0, 0)),
            _resident((None, D_MODEL, 2 * D_FF), lambda i, l: (l[0], 0, 0)),
            pl.BlockSpec((CONV_W, 2 * D_FF), lambda i, l: (0, 0)),
            pl.BlockSpec((1, 2 * D_FF), lambda i, l: (0, 0)),
            _resident((None, D_FF, D_MODEL), lambda i, l: (l[0], 0, 0)),
            pl.BlockSpec((1, D_MODEL), lambda i, l: (0, 0)),
        ],
        out_specs=[
            pl.BlockSpec((tm, D_MODEL), lambda i, l: (i, 0)),
            pl.BlockSpec((1, CONV_W - 1, 2 * D_FF), lambda i, l: (i // tiles_per_batch, 0, 0)),
        ],
        scratch_shapes=[pltpu.VMEM((tm, D_MODEL), BF16), pltpu.VMEM((tm, D_FF), BF16),
                        pltpu.VMEM((CONV_W - 1, 2 * D_FF), F32)])
    body = functools.partial(_ffn_kernel, tiles_per_batch=tiles_per_batch)
    return pl.pallas_call(body, grid_spec=grid_spec,
                          out_shape=(jax.ShapeDtypeStruct(x.shape, F32),
                                     jax.ShapeDtypeStruct((t // seq, CONV_W - 1, 2 * D_FF), F32)),
                          compiler_params=_cparams(1), name="ffn")(
                              layer, x, gpre, w_up, conv_w, conv_b, w_down, gpost)


def _ffn_sample_kernel(l_ref, x_ref, gpre_ref, wup_ref, cw_ref, cb_ref, wdn_ref, gpost_ref, buf_ref,
                       o_ref, tail_ref, act_ref):
    x = x_ref[...]
    h = _rms(x, gpre_ref[...]).astype(BF16)

    def conv(u, cols):
        u2 = buf_ref[:, 0, cols]
        u1 = buf_ref[:, 1, cols]
        tail_ref[:, 0, cols] = u1
        tail_ref[:, 1, cols] = u
        return cb_ref[:, cols] + u2 * cw_ref[0:1, cols] + u1 * cw_ref[1:2, cols] + u * cw_ref[2:3, cols]

    for j in range(D_FF // FC):
        ca_cols = slice(j * FC, (j + 1) * FC)
        cb_cols = slice(D_FF + j * FC, D_FF + (j + 1) * FC)
        ca = conv(_dot(h, wup_ref[:, ca_cols]), ca_cols)
        cb = conv(_dot(h, wup_ref[:, cb_cols]), cb_cols)
        act_ref[:, ca_cols] = (ca * _sigmoid(ca) * cb).astype(BF16)
    o_ref[...] = x + _rms(_dot(act_ref[...], wdn_ref[...]), gpost_ref[...])


def _ffn_sample(layer, x, gpre, w_up, conv_w, conv_b, w_down, gpost, buf):
    n = x.shape[0]
    grid_spec = pltpu.PrefetchScalarGridSpec(
        num_scalar_prefetch=1, grid=(1,),
        in_specs=[
            pl.BlockSpec((n, D_MODEL), lambda i, l: (0, 0)),
            pl.BlockSpec((1, D_MODEL), lambda i, l: (0, 0)),
            _resident((None, D_MODEL, 2 * D_FF), lambda i, l: (l[0], 0, 0)),
            pl.BlockSpec((CONV_W, 2 * D_FF), lambda i, l: (0, 0)),
            pl.BlockSpec((1, 2 * D_FF), lambda i, l: (0, 0)),
            _resident((None, D_FF, D_MODEL), lambda i, l: (l[0], 0, 0)),
            pl.BlockSpec((1, D_MODEL), lambda i, l: (0, 0)),
            pl.BlockSpec((None, n, CONV_W - 1, 2 * D_FF), lambda i, l: (l[0], 0, 0, 0)),
        ],
        out_specs=[pl.BlockSpec((n, D_MODEL), lambda i, l: (0, 0)),
                   pl.BlockSpec((n, CONV_W - 1, 2 * D_FF), lambda i, l: (0, 0, 0))],
        scratch_shapes=[pltpu.VMEM((n, D_FF), BF16)])
    return pl.pallas_call(_ffn_sample_kernel, grid_spec=grid_spec,
                          out_shape=(jax.ShapeDtypeStruct(x.shape, F32),
                                     jax.ShapeDtypeStruct((n, CONV_W - 1, 2 * D_FF), F32)),
                          compiler_params=_cparams(1), name="ffn_sample")(
                              layer, x, gpre, w_up, conv_w, conv_b, w_down, gpost, buf)


def _s_attn_kernel(l_ref, q_ref, kv_ref, ckt_ref, cvt_ref, sink_ref, o_ref):
    bt = q_ref.shape[0]
    inst = [(bi, g) for bi in range(bt) for g in range(N_KV)]
    hs = [slice(g * GQA_G, (g + 1) * GQA_G) for g in range(N_KV)]
    qs = [q_ref[bi, hs[g], :] for bi, g in inst]
    s = [_dot(q.astype(BF16), ckt_ref[bi, g].astype(BF16)) for q, (bi, g) in zip(qs, inst)]
    s_new = [jnp.sum(q * kv_ref[bi:bi + 1, g * HEAD_DIM:(g + 1) * HEAD_DIM], axis=-1, keepdims=True)
             for q, (bi, g) in zip(qs, inst)]
    sk = [sink_ref[hs[g], :] for g in range(N_KV)]
    m = [jnp.maximum(jnp.maximum(jnp.max(a, axis=-1, keepdims=True), b), sk[g])
         for a, b, (_, g) in zip(s, s_new, inst)]
    p = [jnp.exp(a - mm) for a, mm in zip(s, m)]
    p_new = [jnp.exp(b - mm) for b, mm in zip(s_new, m)]
    den = [jnp.sum(pp, axis=-1, keepdims=True) + pn + jnp.exp(sk[g] - mm)
           for pp, pn, mm, (_, g) in zip(p, p_new, m, inst)]
    for pp, pn, dd, (bi, g) in zip(p, p_new, den, inst):
        v_new = kv_ref[bi:bi + 1, KV_W + g * HEAD_DIM:KV_W + (g + 1) * HEAD_DIM]
        o = _dot_nt(pp.astype(BF16), cvt_ref[bi, g].astype(BF16)) + pn * v_new
        o_ref[bi, hs[g], :] = o / dd


def _s_attn(layer, q, kv, cache_kt, cache_vt, sink_col):
    n = q.shape[0]
    w = cache_kt.shape[-1]
    grid_spec = pltpu.PrefetchScalarGridSpec(
        num_scalar_prefetch=1, grid=(n // S_BT,),
        in_specs=[
            pl.BlockSpec((S_BT, N_HEADS, HEAD_DIM), lambda i, l: (i, 0, 0)),
            pl.BlockSpec((S_BT, 2 * KV_W), lambda i, l: (i, 0)),
            pl.BlockSpec((None, S_BT, N_KV, HEAD_DIM, w), lambda i, l: (l[0], i, 0, 0, 0)),
            pl.BlockSpec((None, S_BT, N_KV, HEAD_DIM, w), lambda i, l: (l[0], i, 0, 0, 0)),
            pl.BlockSpec((N_HEADS, 1), lambda i, l: (0, 0)),
        ],
        out_specs=pl.BlockSpec((S_BT, N_HEADS, HEAD_DIM), lambda i, l: (i, 0, 0)))
    return pl.pallas_call(_s_attn_kernel, grid_spec=grid_spec,
                          out_shape=jax.ShapeDtypeStruct((n, N_HEADS, HEAD_DIM), F32),
                          compiler_params=_cparams(1), name="s_attn")(
                              layer, q.reshape(n, N_HEADS, HEAD_DIM), kv, cache_kt, cache_vt, sink_col
                          ).reshape(n, ATTN_W)


def _s_hgrn_kernel(l_ref, r_ref, rt_ref, st_ref, lbt_ref, gn_ref, alias_ref, o_ref, so_ref):
    del alias_ref
    bt = r_ref.shape[0]
    for h in range(R_HEADS):
        sl = slice(h * R_DK, (h + 1) * R_DK)
        zt = rt_ref[R_KW + h * R_DK:R_KW + (h + 1) * R_DK, :]
        lbc = lbt_ref[sl, :]
        logf, kt = _forget_and_key(zt, lbc)
        f = jnp.exp(logf)
        qr = r_ref[:, sl]
        q16 = (qr * _sigmoid(qr)).astype(BF16)
        vs = slice(2 * R_KW + h * R_DV, 2 * R_KW + (h + 1) * R_DV)
        gs = slice(2 * R_KW + R_VW + h * R_DV, 2 * R_KW + R_VW + (h + 1) * R_DV)
        s_new = [f[:, bi:bi + 1] * st_ref[bi, h] + kt[:, bi:bi + 1] * r_ref[bi:bi + 1, vs]
                 for bi in range(bt)]
        for bi in range(bt):
            so_ref[bi, h] = s_new[bi]
        o = jnp.concatenate([_dot(q16[bi:bi + 1, :], s_new[bi].astype(BF16))
                             for bi in range(bt)], axis=0)
        on = o * lax.rsqrt(jnp.mean(o * o, axis=-1, keepdims=True) + EPS) * gn_ref[:, sl]
        gr = r_ref[:, gs]
        o_ref[:, sl] = (on * (gr * _sigmoid(gr))).astype(o_ref.dtype)


def _s_hgrn(layer, r, state, lb_col, gn, new_states):
    n = r.shape[0]
    rt = r.reshape(n // S_BT, S_BT, R_IN_W).transpose(0, 2, 1)
    state_spec = pl.BlockSpec((None, S_BT, R_HEADS, R_DK, R_DV), lambda i, l: (l[0], i, 0, 0, 0))
    grid_spec = pltpu.PrefetchScalarGridSpec(
        num_scalar_prefetch=1, grid=(n // S_BT,),
        in_specs=[
            pl.BlockSpec((S_BT, R_IN_W), lambda i, l: (i, 0)),
            pl.BlockSpec((None, R_IN_W, S_BT), lambda i, l: (i, 0, 0)),
            state_spec,
            pl.BlockSpec((R_KW, 1), lambda i, l: (0, 0)),
            pl.BlockSpec((1, R_VW), lambda i, l: (0, 0)),
            pl.BlockSpec(memory_space=pl.ANY),
        ],
        out_specs=[pl.BlockSpec((S_BT, R_VW), lambda i, l: (i, 0)), state_spec])
    return pl.pallas_call(_s_hgrn_kernel, grid_spec=grid_spec,
                          out_shape=(jax.ShapeDtypeStruct((n, R_VW), F32),
                                     jax.ShapeDtypeStruct(new_states.shape, F32)),
                          input_output_aliases={6: 1},
                          compiler_params=_cparams(1), name="s_hgrn")(
                              layer, r, rt, state, lb_col, gn, new_states)


def _s_xattn_kernel(l_ref, x_ref, gpre_ref, wq_ref, ck_ref, cv_ref, wo_ref, gpost_ref, o_ref):
    bt, n_mem = ck_ref.shape[0], ck_ref.shape[1]
    n_dt = X_HEAD_DIM // LANES
    ones = jnp.ones((LANES, LANES), BF16)
    x = x_ref[...]
    h = _rms(x, gpre_ref[...]).astype(BF16)
    q = _dot(h, wq_ref[...]) * (X_HEAD_DIM ** -0.5)
    rows = []
    for bi in range(bt):
        qrep = jnp.concatenate(
            [q[bi:bi + 1, hd * X_HEAD_DIM + dt * LANES:hd * X_HEAD_DIM + (dt + 1) * LANES]
             for dt in range(n_dt) for hd in range(X_HEADS)], axis=0)
        prod = (ck_ref[bi] * qrep[None]).reshape(n_mem * X_ROWS, LANES).astype(BF16)
        part = _dot(prod, ones).reshape(n_mem, X_ROWS, LANES)
        s = part + pltpu.roll(part, X_HEADS, 1)
        p = jnp.exp(s - jnp.max(s, axis=0, keepdims=True))
        o = jnp.sum(p * cv_ref[bi], axis=0) / jnp.sum(p, axis=0)
        rows.append(jnp.concatenate([o[dt * X_HEADS + hd:dt * X_HEADS + hd + 1, :]
                                     for hd in range(X_HEADS) for dt in range(n_dt)], axis=1))
    y = _dot(jnp.concatenate(rows, axis=0).astype(BF16), wo_ref[...])
    o_ref[...] = x + _rms(y, gpost_ref[...])


def _s_xattn(layer, x, gpre, w_xq, cache_k, cache_v, w_xo, gpost):
    n = x.shape[0]
    grid_spec = pltpu.PrefetchScalarGridSpec(
        num_scalar_prefetch=1, grid=(n // X_BT,),
        in_specs=[
            pl.BlockSpec((X_BT, D_MODEL), lambda i, l: (i, 0)),
            pl.BlockSpec((1, D_MODEL), lambda i, l: (0, 0)),
            _resident((None, D_MODEL, D_MODEL), lambda i, l: (l[0], 0, 0)),
            pl.BlockSpec((None, X_BT, N_MEM, X_ROWS, LANES), lambda i, l: (l[0], i, 0, 0, 0)),
            pl.BlockSpec((None, X_BT, N_MEM, X_ROWS, LANES), lambda i, l: (l[0], i, 0, 0, 0)),
            _resident((None, D_MODEL, D_MODEL), lambda i, l: (l[0], 0, 0)),
            pl.BlockSpec((1, D_MODEL), lambda i, l: (0, 0)),
        ],
        out_specs=pl.BlockSpec((X_BT, D_MODEL), lambda i, l: (i, 0)))
    return pl.pallas_call(_s_xattn_kernel, grid_spec=grid_spec,
                          out_shape=jax.ShapeDtypeStruct(x.shape, F32),
                          compiler_params=_cparams(1), name="s_xattn")(
                              layer, x, gpre, w_xq, cache_k, cache_v, w_xo, gpost)


def kernel(x_prompt, x_sample, cache_win_k, cache_win_v, cache_mem_k, cache_mem_v, state_hgrn, cache_ffn_conv, mem_prompt, w_in, w_o, attn_sinks, lb_logits, hgrn_norm, w_xq, w_xk, w_xv, w_xo, w_up, conv_w, conv_b, w_down, g_pre_mix, g_post_mix, g_pre_x, g_post_x, g_mem, g_pre_ffn, g_post_ffn):
    batch, seq, _ = x_prompt.shape
    n_s, dec_seq, _ = x_sample.shape
    depth = w_in.shape[0]
    assert dec_seq == 1 and seq % WINDOW == 0 and seq % HC == 0
    assert cache_win_k.shape[2] == WINDOW and n_s % S_BT == 0 and n_s % X_BT == 0
    tm = min(TM, seq)
    assert seq % tm == 0

    w_in_b, w_o_b, w_xq_b, w_xk_b, w_xv_b, w_xo_b, w_up_b, w_down_b = (
        w.astype(BF16) for w in (w_in, w_o, w_xq, w_xk, w_xv, w_xo, w_up, w_down))
    lb, cos_p, sin_p, cos_s, sin_s = _prep(lb_logits, seq, n_s)
    mk5, mv5, mk, mv = _memkv(mem_prompt.reshape(batch * N_MEM, D_MODEL), g_mem.reshape(depth, 1, D_MODEL),
                              w_xk_b, w_xv_b)

    cwk = jnp.transpose(cache_win_k, (0, 1, 3, 4, 2))
    cwv = jnp.transpose(cache_win_v, (0, 1, 3, 4, 2))
    def tile_view(c):
        c = c.reshape(depth, n_s, N_MEM, X_HEADS, X_HEAD_DIM // LANES, LANES)
        return jnp.transpose(c, (0, 1, 2, 4, 3, 5)).reshape(depth, n_s, N_MEM, X_ROWS, LANES)

    cmk, cmv = tile_view(cache_mem_k), tile_view(cache_mem_v)

    xp = x_prompt.reshape(batch * seq, D_MODEL)
    xs = x_sample.reshape(n_s, D_MODEL)
    st_s = jnp.zeros(state_hgrn.shape, F32)
    per_layer = []
    for l in range(depth):
        layer = jnp.full((1,), l, jnp.int32)
        sink, lb_l, gn_l, cb_l = (a[l:l + 1] for a in (attn_sinks, lb, hgrn_norm, conv_b))
        cw_l = conv_w[l]
        gpm, gom, gpx, gox, gpf, gof = (g[l:l + 1] for g in (g_pre_mix, g_post_mix, g_pre_x, g_post_x,
                                                            g_pre_ffn, g_post_ffn))

        q_p, kv_p, r_p = _proj_in(layer, xp, gpm, w_in_b, cos_p, sin_p, tm, BF16)
        attn_p = _swa(q_p, kv_p, sink, seq)
        rec_p, st_p = _hgrn(r_p, lb_l, gn_l, batch, seq)
        xp = _mix_xattn(layer, attn_p, rec_p, w_o_b, xp, gom, gpx, w_xq_b, mk, mv, w_xo_b, gox, seq, tm)

        q_s, kv_s, r_s = _proj_in(layer, xs, gpm, w_in_b, cos_s, sin_s, n_s, F32)
        attn_s = _s_attn(layer, q_s, kv_s, cwk, cwv, sink.reshape(N_HEADS, 1))
        rec_s, st_s = _s_hgrn(layer, r_s, state_hgrn, lb_l.reshape(R_KW, 1), gn_l, st_s)
        xs = _mix_out(layer, attn_s, rec_s, w_o_b, xs, gom, n_s)
        xs = _s_xattn(layer, xs, gpx, w_xq_b, cmk, cmv, w_xo_b, gox)

        xp, tail_p = _ffn(layer, xp, gpf, w_up_b, cw_l, cb_l, w_down_b, gof, seq, tm)
        xs, tail_s = _ffn_sample(layer, xs, gpf, w_up_b, cw_l, cb_l, w_down_b, gof, cache_ffn_conv)

        kv_keep = kv_p.reshape(batch, seq, 2 * KV_W)[:, seq - WINDOW:, :]
        per_layer.append((kv_keep[..., :KV_W], kv_keep[..., KV_W:], kv_s[:, :KV_W], kv_s[:, KV_W:],
                          st_p, tail_p, tail_s))

    wk_p, wv_p, wk_s, wv_s, st_p, tail_p, tail_s = (jnp.stack(a) for a in zip(*per_layer))
    return (xp.reshape(batch, seq, D_MODEL), xs.reshape(n_s, 1, D_MODEL),
            wk_p.reshape(depth, batch, WINDOW, N_KV, HEAD_DIM), wv_p.reshape(depth, batch, WINDOW, N_KV, HEAD_DIM),
            wk_s.reshape(depth, n_s, 1, N_KV, HEAD_DIM), wv_s.reshape(depth, n_s, 1, N_KV, HEAD_DIM),
            mk5.reshape(depth, batch, N_MEM, X_HEADS, X_HEAD_DIM), mv5.reshape(depth, batch, N_MEM, X_HEADS, X_HEAD_DIM),
            st_p, st_s,
            tail_p, tail_s)
```

```python
import functools
import math

import jax
import jax.numpy as jnp
from jax import lax
from jax.experimental import pallas as pl
from jax.experimental.pallas import tpu as pltpu

F32 = jnp.float32
BF16 = jnp.bfloat16

D_MODEL = 1024
HEAD_DIM = 64
N_HEADS = 8
N_KV = 2
GQA_G = N_HEADS // N_KV
WINDOW = 128
ROPE_THETA = 10000.0
R_HEADS = 4
R_DK = 128
R_DV = 128
ATTN_W = N_HEADS * HEAD_DIM
KV_W = N_KV * HEAD_DIM
R_KW = R_HEADS * R_DK
R_VW = R_HEADS * R_DV
R_IN_W = 2 * R_KW + 2 * R_VW
P_IN = ATTN_W + 2 * KV_W + R_IN_W
N_MEM = 256
X_HEADS = 4
X_HEAD_DIM = D_MODEL // X_HEADS
X_ROWS = X_HEADS * (X_HEAD_DIM // 128)
D_FF = 2816
CONV_W = 3
EPS = 1e-6
PAST_LEN = 8192

LANES = 128
SUBLANES = 8
VMEM_LIMIT = 56 * 1024 * 1024

TM = 1024
SWA_TQ = 1024
HC = 128
HG_ROWS = 2048
LOG2_E = 1.4426950408889634
FC = 256
S_BT = 16
X_BT = 8


def _cparams(n_axes):
    return pltpu.CompilerParams(dimension_semantics=("arbitrary",) * n_axes,
                                vmem_limit_bytes=VMEM_LIMIT)


def _rms(x, g):
    return x * lax.rsqrt(jnp.mean(x * x, axis=-1, keepdims=True) + EPS) * g


def _sigmoid(x):
    return 1.0 / (1.0 + jnp.exp(-x))


def _dot(a, b):
    return jnp.dot(a, b, preferred_element_type=F32)


def _dot_nt(a, b):
    return lax.dot_general(a, b, (((1,), (1,)), ((), ())), preferred_element_type=F32)


def _dot_tn(a, b):
    return lax.dot_general(a, b, (((0,), (0,)), ((), ())), preferred_element_type=F32)


def _resident(block_shape, index_map):
    return pl.BlockSpec(block_shape, index_map, pipeline_mode=pl.Buffered(1))


def _prep_kernel(lbl_ref, lb_ref, cos_ref, sin_ref, cos_s_ref, sin_s_ref):
    x = lbl_ref[...]
    e = jnp.exp(x - jnp.max(x, axis=0, keepdims=True))
    sm = e / jnp.sum(e, axis=0, keepdims=True)
    depth = x.shape[0]
    acc = jnp.zeros((1, x.shape[1]), F32)
    rows = [acc]
    for l in range(1, depth):
        acc = acc + sm[l:l + 1, :]
        rows.append(acc)
    lb_ref[...] = jnp.concatenate(rows, axis=0)

    half = HEAD_DIM // 2

    def tables(shape, pos):
        lane = lax.broadcasted_iota(jnp.int32, shape, 1)
        j = (lane & (half - 1)).astype(F32)
        inv_freq = jnp.exp(j * (-math.log(ROPE_THETA) / half))
        ang = pos * inv_freq
        first = (lane & (HEAD_DIM - 1)) < half
        return jnp.cos(ang), jnp.where(first, -jnp.sin(ang), jnp.sin(ang))

    pos_p = lax.broadcasted_iota(jnp.int32, cos_ref.shape, 0).astype(F32)
    c, s = tables(cos_ref.shape, pos_p)
    cos_ref[...] = c
    sin_ref[...] = s
    c, s = tables(cos_s_ref.shape, jnp.full(cos_s_ref.shape, float(PAST_LEN), F32))
    cos_s_ref[...] = c
    sin_s_ref[...] = s


def _prep(lb_logits, seq, n_sample):
    depth = lb_logits.shape[0]
    out_shape = (jax.ShapeDtypeStruct((depth, R_KW), F32),
                 jax.ShapeDtypeStruct((seq, LANES), F32), jax.ShapeDtypeStruct((seq, LANES), F32),
                 jax.ShapeDtypeStruct((n_sample, LANES), F32), jax.ShapeDtypeStruct((n_sample, LANES), F32))
    return pl.pallas_call(_prep_kernel, out_shape=out_shape, name="prep")(lb_logits)


def _rope(x, cos, sin):
    w = x.shape[-1]
    reps = w // LANES
    if reps > 1:
        cos = jnp.tile(cos, (1, reps))
        sin = jnp.tile(sin, (1, reps))
    half = HEAD_DIM // 2
    lane = lax.broadcasted_iota(jnp.int32, x.shape, 1)
    first = (lane & (HEAD_DIM - 1)) < half
    swapped = jnp.where(first, pltpu.roll(x, w - half, 1), pltpu.roll(x, half, 1))
    return x * cos + swapped * sin


def _proj_in_kernel(l_ref, x_ref, g_ref, w_ref, cos_ref, sin_ref, q_ref, kv_ref, r_ref):
    h = _rms(x_ref[...], g_ref[...]).astype(BF16)
    cos = cos_ref[...]
    sin = sin_ref[...]
    q = _dot(h, w_ref[:, 0:ATTN_W])
    q_ref[...] = (_rope(q, cos, sin) * (HEAD_DIM ** -0.5)).astype(q_ref.dtype)
    k = _dot(h, w_ref[:, ATTN_W:ATTN_W + KV_W])
    kv_ref[:, 0:KV_W] = _rope(k, cos, sin)
    kv_ref[:, KV_W:2 * KV_W] = _dot(h, w_ref[:, ATTN_W + KV_W:ATTN_W + 2 * KV_W])
    r_ref[...] = _dot(h, w_ref[:, ATTN_W + 2 * KV_W:P_IN])


def _proj_in(layer, x, g, w_in, cos, sin, tm, q_dtype):
    t = x.shape[0]
    n_tab = cos.shape[0] // tm
    grid_spec = pltpu.PrefetchScalarGridSpec(
        num_scalar_prefetch=1, grid=(t // tm,),
        in_specs=[
            pl.BlockSpec((tm, D_MODEL), lambda i, l: (i, 0)),
            pl.BlockSpec((1, D_MODEL), lambda i, l: (0, 0)),
            _resident((None, D_MODEL, P_IN), lambda i, l: (l[0], 0, 0)),
            pl.BlockSpec((tm, LANES), lambda i, l: (i % n_tab, 0)),
            pl.BlockSpec((tm, LANES), lambda i, l: (i % n_tab, 0)),
        ],
        out_specs=[
            pl.BlockSpec((tm, ATTN_W), lambda i, l: (i, 0)),
            pl.BlockSpec((tm, 2 * KV_W), lambda i, l: (i, 0)),
            pl.BlockSpec((tm, R_IN_W), lambda i, l: (i, 0)),
        ])
    out_shape = (jax.ShapeDtypeStruct((t, ATTN_W), q_dtype),
                 jax.ShapeDtypeStruct((t, 2 * KV_W), F32),
                 jax.ShapeDtypeStruct((t, R_IN_W), F32))
    return pl.pallas_call(_proj_in_kernel, grid_spec=grid_spec, out_shape=out_shape,
                          compiler_params=_cparams(1), name="proj_in")(layer, x, g, w_in, cos, sin)


def _swa_kernel(q_ref, kv_ref, kvp_ref, sink_ref, o_ref, *, tiles_per_batch):
    i = pl.program_id(0)
    w = WINDOW
    nblk = q_ref.shape[0] // w
    first_key = jnp.where(i % tiles_per_batch == 0, w, 0)
    row = lax.broadcasted_iota(jnp.int32, (w, 2 * w), 0)
    col = lax.broadcasted_iota(jnp.int32, (w, 2 * w), 1)
    band = (col >= row) & (col <= row + w)
    low_o = lax.broadcasted_iota(jnp.int32, (w, LANES), 1) < HEAD_DIM
    low_v = lax.broadcasted_iota(jnp.int32, (2 * w, LANES), 1) < HEAD_DIM
    zeros_half = jnp.zeros((HEAD_DIM, 2 * w), BF16)
    valid, k_pad, vcat = [], [], []
    for b in range(nblk):
        if b == 0:
            kv2 = jnp.concatenate([kvp_ref[...], kv_ref[0:w, :]], axis=0)
            valid.append(band & (col >= first_key))
        else:
            kv2 = kv_ref[(b - 1) * w:(b + 1) * w, :]
            valid.append(band)
        kt = kv2[:, 0:KV_W].T.astype(BF16)
        v2 = kv2[:, KV_W:2 * KV_W]
        v_sw = pltpu.roll(v2, HEAD_DIM, 1)
        for g in range(N_KV):
            kg = kt[g * HEAD_DIM:(g + 1) * HEAD_DIM, :]
            k_pad.append((jnp.concatenate([kg, zeros_half], axis=0), jnp.concatenate([zeros_half, kg], axis=0)))
            if g == 0:
                v_lo, v_hi = jnp.where(low_v, v2, 0.0), jnp.where(low_v, 0.0, v_sw)
            else:
                v_lo, v_hi = jnp.where(low_v, v_sw, 0.0), jnp.where(low_v, 0.0, v2)
            vcat.append(jnp.concatenate([v_lo, v_hi], axis=0).astype(BF16))

    inst = [(b, h) for b in range(nblk) for h in range(N_HEADS)]
    scores = [jnp.where(valid[b],
                        _dot(q_ref[b * w:(b + 1) * w, (h // 2) * LANES:(h // 2 + 1) * LANES],
                             k_pad[b * N_KV + h // GQA_G][h % 2]),
                        -jnp.inf) for b, h in inst]
    sinks = [sink_ref[0:1, h:h + 1] for _, h in inst]
    maxes = [jnp.maximum(jnp.max(s, axis=-1, keepdims=True), sk) for s, sk in zip(scores, sinks)]
    probs = [jnp.exp(s - m) for s, m in zip(scores, maxes)]
    rden = [1.0 / (jnp.sum(p, axis=-1, keepdims=True) + jnp.exp(sk - m))
            for p, sk, m in zip(probs, sinks, maxes)]
    for b in range(nblk):
        for pr in range(N_HEADS // 2):
            lo = b * N_HEADS + 2 * pr
            ps = slice(pr * LANES, (pr + 1) * LANES)
            pcat = jnp.concatenate([probs[lo].astype(BF16), probs[lo + 1].astype(BF16)], axis=1)
            o = _dot(pcat, vcat[b * N_KV + (2 * pr) // GQA_G])
            o_ref[b * w:(b + 1) * w, ps] = (o * jnp.where(low_o, rden[lo], rden[lo + 1])).astype(o_ref.dtype)


def _swa(q, kv, sink, seq):
    t = q.shape[0]
    tq = min(SWA_TQ, seq)
    nblk = tq // WINDOW
    body = functools.partial(_swa_kernel, tiles_per_batch=seq // tq)
    return pl.pallas_call(
        body, grid=(t // tq,),
        in_specs=[
            pl.BlockSpec((tq, ATTN_W), lambda i: (i, 0)),
            pl.BlockSpec((tq, 2 * KV_W), lambda i: (i, 0)),
            pl.BlockSpec((WINDOW, 2 * KV_W), lambda i: (jnp.maximum(i * nblk - 1, 0), 0)),
            pl.BlockSpec((1, N_HEADS), lambda i: (0, 0)),
        ],
        out_specs=pl.BlockSpec((tq, ATTN_W), lambda i: (i, 0)),
        out_shape=jax.ShapeDtypeStruct(q.shape, BF16),
        compiler_params=_cparams(1), name="swa")(q, kv, kv, sink)


def _forget_and_key(z, lb):
    e = jnp.exp(-jnp.abs(z))
    logsig = jnp.minimum(z, 0.0) - jnp.log(1.0 + e)
    a = jnp.log(lb)
    c = jnp.log(1.0 - lb) + logsig
    logf = jnp.maximum(a, c) + jnp.log(1.0 + jnp.exp(-jnp.abs(a - c)))
    key = (1.0 - lb) * (jnp.where(z >= 0.0, e, 1.0) / (1.0 + e))
    return logf, key


def _cumsum_rows(g, tril):
    hi = g.astype(BF16)
    r1 = g - hi.astype(F32)
    mid = r1.astype(BF16)
    lo = (r1 - mid.astype(F32)).astype(BF16)
    return _dot(tril, hi) + _dot(tril, mid) + _dot(tril, lo)


def _level_ref(b, half):
    n_rows, width = b.shape
    n = 2 * half
    if n >= 2 * SUBLANES:
        pieces = [jnp.broadcast_to(b[i * n + half - 1:i * n + half, :], (n, width))
                  for i in range(n_rows // n)]
        return pieces[0] if len(pieces) == 1 else jnp.concatenate(pieces, axis=0)
    if half == 1:
        odd = (lax.broadcasted_iota(jnp.int32, b.shape, 0) & 1) == 1
        return jnp.where(odd, pltpu.roll(b, 1, 0), b)
    b3 = b.reshape(n_rows // SUBLANES, SUBLANES, width)
    sub = lax.broadcasted_iota(jnp.int32, b3.shape, 1)

    def bcast(r):
        return jnp.broadcast_to(b3[:, r:r + 1, :], b3.shape)

    ref = bcast(3) if half == 4 else jnp.where(sub < 4, bcast(1), bcast(5))
    return ref.reshape(n_rows, width)


def _hgrn_kernel(qr_ref, fr_ref, ir_ref, gr_ref, lb_ref, gn_ref, o_ref, s_out_ref, st_ref):
    c = pl.program_id(1)
    last = pl.num_programs(1) - 1

    @pl.when(c == 0)
    def _():
        st_ref[...] = jnp.zeros_like(st_ref)

    rr = lax.broadcasted_iota(jnp.int32, (HC, HC), 0)
    cc = lax.broadcasted_iota(jnp.int32, (HC, HC), 1)
    xor = rr ^ cc
    causal = cc <= rr
    tril = jnp.where(causal, 1.0, 0.0).astype(BF16)
    halves = [1 << i for i in range(HC.bit_length() - 1)]
    row_w = lax.broadcasted_iota(jnp.int32, (HC, R_KW), 0)
    signs = [jnp.where((row_w & half) != 0, 1.0, -1.0) for half in halves]
    heads = [slice(h * R_DK, (h + 1) * R_DK) for h in range(R_HEADS)]
    st = [st_ref[h] for h in range(R_HEADS)]

    for ci in range(qr_ref.shape[0] // HC):
        rows = slice(ci * HC, (ci + 1) * HC)
        g, k = _forget_and_key(fr_ref[rows, :], lb_ref[...])
        qr = qr_ref[rows, :]
        q = qr * _sigmoid(qr)
        v16 = ir_ref[rows, :].astype(BF16)
        b = _cumsum_rows(g * LOG2_E, tril)
        q16 = q.astype(BF16)
        k16 = k.astype(BF16)
        b_last = b[HC - 1:HC, :]
        q_in = (q * jnp.exp2(b)).astype(BF16)
        k_out = (k * jnp.exp2(b_last - b)).astype(BF16)
        decay = jnp.exp2(b_last)
        gr = gr_ref[rows, :]
        gate = gr * _sigmoid(gr)
        q_lv, k_lv = [], []
        for half, sgn in zip(halves, signs):
            e = jnp.exp2((b - _level_ref(b, half)) * sgn).astype(BF16)
            q_lv.append(q16 * e)
            k_lv.append(k16 * e)

        diag = [_dot_nt(q16[:, sl], k16[:, sl]) for sl in heads]
        levels = [[_dot_nt(ql[:, sl], kl[:, sl]) for sl in heads] for ql, kl in zip(q_lv, k_lv)]
        att16 = []
        for h in range(R_HEADS):
            att = diag[h]
            for half, lv in zip(halves, levels):
                att = jnp.where(xor >= half, lv[h], att)
            att16.append(jnp.where(causal, att, 0.0).astype(BF16))
        outs = [_dot(att16[h], v16[:, sl]) + _dot_nt(q_in[:, sl], st[h].astype(BF16))
                for h, sl in enumerate(heads)]
        st = [st[h] * decay[:, sl] + _dot_tn(v16[:, sl], k_out[:, sl]) for h, sl in enumerate(heads)]
        for h, sl in enumerate(heads):
            o = outs[h]
            on = o * lax.rsqrt(jnp.mean(o * o, axis=-1, keepdims=True) + EPS) * gn_ref[:, sl]
            o_ref[rows, sl] = (on * gate[:, sl]).astype(o_ref.dtype)

    for h in range(R_HEADS):
        st_ref[h] = st[h]

    @pl.when(c == last)
    def _():
        for h in range(R_HEADS):
            s_out_ref[0, h] = st[h].T


def _hgrn(r, lb, gn, batch, seq):
    rows = min(HG_ROWS, seq)
    nc = seq // rows
    t = r.shape[0]

    def col(kk):
        return pl.BlockSpec((rows, R_KW), lambda b, c: (b * nc + c, kk))

    return pl.pallas_call(
        _hgrn_kernel, grid=(batch, nc),
        in_specs=[col(0), col(1), col(2), col(3),
                  pl.BlockSpec((1, R_KW), lambda b, c: (0, 0)),
                  pl.BlockSpec((1, R_VW), lambda b, c: (0, 0))],
        out_specs=[pl.BlockSpec((rows, R_VW), lambda b, c: (b * nc + c, 0)),
                   pl.BlockSpec((1, R_HEADS, R_DK, R_DV), lambda b, c: (b, 0, 0, 0))],
        out_shape=(jax.ShapeDtypeStruct((t, R_VW), BF16),
                   jax.ShapeDtypeStruct((batch, R_HEADS, R_DK, R_DV), F32)),
        scratch_shapes=[pltpu.VMEM((R_HEADS, R_DV, R_DK), F32)],
        compiler_params=_cparams(2), name="hgrn")(r, r, r, r, lb, gn)


def _mix_out_kernel(l_ref, a_ref, r_ref, w_ref, x_ref, g_ref, o_ref):
    m = (_dot(a_ref[...].astype(BF16), w_ref[0:ATTN_W, :])
         + _dot(r_ref[...].astype(BF16), w_ref[ATTN_W:ATTN_W + R_VW, :]))
    o_ref[...] = x_ref[...] + _rms(m, g_ref[...])


def _mix_out(layer, attn, rec, w_o, x, g, tm):
    t = x.shape[0]
    grid_spec = pltpu.PrefetchScalarGridSpec(
        num_scalar_prefetch=1, grid=(t // tm,),
        in_specs=[
            pl.BlockSpec((tm, ATTN_W), lambda i, l: (i, 0)),
            pl.BlockSpec((tm, R_VW), lambda i, l: (i, 0)),
            _resident((None, ATTN_W + R_VW, D_MODEL), lambda i, l: (l[0], 0, 0)),
            pl.BlockSpec((tm, D_MODEL), lambda i, l: (i, 0)),
            pl.BlockSpec((1, D_MODEL), lambda i, l: (0, 0)),
        ],
        out_specs=pl.BlockSpec((tm, D_MODEL), lambda i, l: (i, 0)))
    return pl.pallas_call(_mix_out_kernel, grid_spec=grid_spec,
                          out_shape=jax.ShapeDtypeStruct(x.shape, F32),
                          compiler_params=_cparams(1), name="mix_out")(layer, attn, rec, w_o, x, g)


def _memkv_kernel(m_ref, g_ref, wk_ref, wv_ref, k_ref, v_ref, k16_ref, v16_ref):
    h = _rms(m_ref[...], g_ref[...]).astype(BF16)
    for w_ref, o_ref, o16_ref in ((wk_ref, k_ref, k16_ref), (wv_ref, v_ref, v16_ref)):
        y = _dot(h, w_ref[...])
        o16_ref[...] = y.astype(BF16)
        for hd in range(X_HEADS):
            o_ref[:, hd, :] = y[:, hd * X_HEAD_DIM:(hd + 1) * X_HEAD_DIM]


def _memkv(mem, g_mem, w_xk, w_xv):
    depth = w_xk.shape[0]
    rows = mem.shape[0]
    tm = min(TM, rows)
    out = jax.ShapeDtypeStruct((depth, rows, X_HEADS, X_HEAD_DIM), F32)
    out16 = jax.ShapeDtypeStruct((depth, rows, D_MODEL), BF16)
    return pl.pallas_call(
        _memkv_kernel, grid=(depth, rows // tm),
        in_specs=[
            pl.BlockSpec((tm, D_MODEL), lambda l, i: (i, 0)),
            pl.BlockSpec((None, 1, D_MODEL), lambda l, i: (l, 0, 0)),
            pl.BlockSpec((None, D_MODEL, D_MODEL), lambda l, i: (l, 0, 0)),
            pl.BlockSpec((None, D_MODEL, D_MODEL), lambda l, i: (l, 0, 0)),
        ],
        out_specs=[pl.BlockSpec((None, tm, X_HEADS, X_HEAD_DIM), lambda l, i: (l, i, 0, 0)),
                   pl.BlockSpec((None, tm, X_HEADS, X_HEAD_DIM), lambda l, i: (l, i, 0, 0)),
                   pl.BlockSpec((None, tm, D_MODEL), lambda l, i: (l, i, 0)),
                   pl.BlockSpec((None, tm, D_MODEL), lambda l, i: (l, i, 0))],
        out_shape=(out, out, out16, out16), compiler_params=_cparams(2), name="memkv")(mem, g_mem, w_xk, w_xv)


def _mix_xattn_kernel(l_ref, a_ref, r_ref, wmix_ref, x_ref, gmix_ref,
                      gpre_ref, wq_ref, mk_ref, mv_ref, wo_ref, gpost_ref, o_ref):
    m = (_dot(a_ref[...], wmix_ref[0:ATTN_W, :]) + _dot(r_ref[...], wmix_ref[ATTN_W:ATTN_W + R_VW, :]))
    x = x_ref[...] + _rms(m, gmix_ref[...])
    h = _rms(x, gpre_ref[...]).astype(BF16)
    q = (_dot(h, wq_ref[...]) * (X_HEAD_DIM ** -0.5)).astype(BF16)
    heads = [slice(hd * X_HEAD_DIM, (hd + 1) * X_HEAD_DIM) for hd in range(X_HEADS)]
    s = [_dot_nt(q[:, sl], mk_ref[:, sl]) for sl in heads]
    p = [jnp.exp(a - jnp.max(a, axis=-1, keepdims=True)) for a in s]
    den = [jnp.sum(a, axis=-1, keepdims=True) for a in p]
    outs = [(_dot(a.astype(BF16), mv_ref[:, sl]) / d).astype(BF16) for a, d, sl in zip(p, den, heads)]
    y = _dot(jnp.concatenate(outs, axis=1), wo_ref[...])
    o_ref[...] = x + _rms(y, gpost_ref[...])


def _mix_xattn(layer, attn, rec, w_o, x, gmix, gpre, w_xq, mk, mv, w_xo, gpost, seq, tm):
    t = x.shape[0]
    tpb = seq // tm
    grid_spec = pltpu.PrefetchScalarGridSpec(
        num_scalar_prefetch=1, grid=(t // tm,),
        in_specs=[
            pl.BlockSpec((tm, ATTN_W), lambda i, l: (i, 0)),
            pl.BlockSpec((tm, R_VW), lambda i, l: (i, 0)),
            _resident((None, ATTN_W + R_VW, D_MODEL), lambda i, l: (l[0], 0, 0)),
            pl.BlockSpec((tm, D_MODEL), lambda i, l: (i, 0)),
            pl.BlockSpec((1, D_MODEL), lambda i, l: (0, 0)),
            pl.BlockSpec((1, D_MODEL), lambda i, l: (0, 0)),
            _resident((None, D_MODEL, D_MODEL), lambda i, l: (l[0], 0, 0)),
            pl.BlockSpec((None, N_MEM, D_MODEL), lambda i, l: (l[0], i // tpb, 0)),
            pl.BlockSpec((None, N_MEM, D_MODEL), lambda i, l: (l[0], i // tpb, 0)),
            _resident((None, D_MODEL, D_MODEL), lambda i, l: (l[0], 0, 0)),
            pl.BlockSpec((1, D_MODEL), lambda i, l: (0, 0)),
        ],
        out_specs=pl.BlockSpec((tm, D_MODEL), lambda i, l: (i, 0)))
    return pl.pallas_call(_mix_xattn_kernel, grid_spec=grid_spec,
                          out_shape=jax.ShapeDtypeStruct(x.shape, F32),
                          compiler_params=_cparams(1), name="mix_xattn")(
                              layer, attn, rec, w_o, x, gmix, gpre, w_xq, mk, mv, w_xo, gpost)


def _conv_chunk(u, cols, cw_ref, cb_ref, carry_ref, sub0):
    tm = u.shape[0]
    w0 = cw_ref[0:1, cols]
    w1 = cw_ref[1:2, cols]
    w2 = cw_ref[2:3, cols]

    def shift1(a, first):
        rolled = pltpu.roll(a, 1, 0)
        head = jnp.where(sub0, first, rolled[0:SUBLANES])
        return jnp.concatenate([head, rolled[SUBLANES:]], axis=0)

    p0 = carry_ref[0:1, cols]
    p1 = carry_ref[1:2, cols]
    carry_ref[:, cols] = u[tm - (CONV_W - 1):tm, :]
    v = u * w1 + shift1(u * w0, p1 * w0)
    return cb_ref[:, cols] + u * w2 + shift1(v, p1 * w1 + p0 * w0)


def _ffn_kernel(l_ref, x_ref, gpre_ref, wup_ref, cw_ref, cb_ref, wdn_ref, gpost_ref,
                o_ref, tail_ref, h_ref, act_ref, carry_ref, *, tiles_per_batch):
    i = pl.program_id(0)

    @pl.when(i % tiles_per_batch == 0)
    def _():
        carry_ref[...] = jnp.zeros_like(carry_ref)

    h_ref[...] = _rms(x_ref[...], gpre_ref[...]).astype(BF16)
    sub0 = lax.broadcasted_iota(jnp.int32, (SUBLANES, FC), 0) == 0
    for j in range(D_FF // FC):
        ca_cols = slice(j * FC, (j + 1) * FC)
        cb_cols = slice(D_FF + j * FC, D_FF + (j + 1) * FC)
        ca = _conv_chunk(_dot(h_ref[...], wup_ref[:, ca_cols]), ca_cols, cw_ref, cb_ref, carry_ref, sub0)
        cb = _conv_chunk(_dot(h_ref[...], wup_ref[:, cb_cols]), cb_cols, cw_ref, cb_ref, carry_ref, sub0)
        act_ref[:, ca_cols] = (ca * _sigmoid(ca) * cb).astype(BF16)
    o_ref[...] = x_ref[...] + _rms(_dot(act_ref[...], wdn_ref[...]), gpost_ref[...])

    @pl.when(i % tiles_per_batch == tiles_per_batch - 1)
    def _():
        tail_ref[0] = carry_ref[...]


def _ffn(layer, x, gpre, w_up, conv_w, conv_b, w_down, gpost, seq, tm):
    t = x.shape[0]
    tiles_per_batch = seq // tm
    grid_spec = pltpu.PrefetchScalarGridSpec(
        num_scalar_prefetch=1, grid=(t // tm,),
        in_specs=[
            pl.BlockSpec((tm, D_MODEL), lambda i, l: (i, 0)),
            pl.BlockSpec((1, D_MODEL), lambda i, l: (0, 0)),
            _resident((None, D_MODEL, 2 * D_FF), lambda i, l: (l[0], 0, 0)),
            pl.BlockSpec((CONV_W, 2 * D_FF), lambda i, l: (0, 0)),
            pl.BlockSpec((1, 2 * D_FF), lambda i, l: (0, 0)),
            _resident((None, D_FF, D_MODEL), lambda i, l: (l[0], 0, 0)),
            pl.BlockSpec((1, D_MODEL), lambda i, l: (0, 0)),
        ],
        out_specs=[
            pl.BlockSpec((tm, D_MODEL), lambda i, l: (i, 0)),
            pl.BlockSpec((1, CONV_W - 1, 2 * D_FF), lambda i, l: (i // tiles_per_batch, 0, 0)),
        ],
        scratch_shapes=[pltpu.VMEM((tm, D_MODEL), BF16), pltpu.VMEM((tm, D_FF), BF16),
                        pltpu.VMEM((CONV_W - 1, 2 * D_FF), F32)])
    body = functools.partial(_ffn_kernel, tiles_per_batch=tiles_per_batch)
    return pl.pallas_call(body, grid_spec=grid_spec,
                          out_shape=(jax.ShapeDtypeStruct(x.shape, F32),
                                     jax.ShapeDtypeStruct((t // seq, CONV_W - 1, 2 * D_FF), F32)),
                          compiler_params=_cparams(1), name="ffn")(
                              layer, x, gpre, w_up, conv_w, conv_b, w_down, gpost)


def _ffn_sample_kernel(l_ref, x_ref, gpre_ref, wup_ref, cw_ref, cb_ref, wdn_ref, gpost_ref, buf_ref,
                       o_ref, tail_ref, act_ref):
    x = x_ref[...]
    h = _rms(x, gpre_ref[...]).astype(BF16)

    def conv(u, cols):
        u2 = buf_ref[:, 0, cols]
        u1 = buf_ref[:, 1, cols]
        tail_ref[:, 0, cols] = u1
        tail_ref[:, 1, cols] = u
        return cb_ref[:, cols] + u2 * cw_ref[0:1, cols] + u1 * cw_ref[1:2, cols] + u * cw_ref[2:3, cols]

    for j in range(D_FF // FC):
        ca_cols = slice(j * FC, (j + 1) * FC)
        cb_cols = slice(D_FF + j * FC, D_FF + (j + 1) * FC)
        ca = conv(_dot(h, wup_ref[:, ca_cols]), ca_cols)
        cb = conv(_dot(h, wup_ref[:, cb_cols]), cb_cols)
        act_ref[:, ca_cols] = (ca * _sigmoid(ca) * cb).astype(BF16)
    o_ref[...] = x + _rms(_dot(act_ref[...], wdn_ref[...]), gpost_ref[...])


def _ffn_sample(layer, x, gpre, w_up, conv_w, conv_b, w_down, gpost, buf):
    n = x.shape[0]
    grid_spec = pltpu.PrefetchScalarGridSpec(
        num_scalar_prefetch=1, grid=(1,),
        in_specs=[
            pl.BlockSpec((n, D_MODEL), lambda i, l: (0, 0)),
            pl.BlockSpec((1, D_MODEL), lambda i, l: (0, 0)),
            _resident((None, D_MODEL, 2 * D_FF), lambda i, l: (l[0], 0, 0)),
            pl.BlockSpec((CONV_W, 2 * D_FF), lambda i, l: (0, 0)),
            pl.BlockSpec((1, 2 * D_FF), lambda i, l: (0, 0)),
            _resident((None, D_FF, D_MODEL), lambda i, l: (l[0], 0, 0)),
            pl.BlockSpec((1, D_MODEL), lambda i, l: (0, 0)),
            pl.BlockSpec((None, n, CONV_W - 1, 2 * D_FF), lambda i, l: (l[0], 0, 0, 0)),
        ],
        out_specs=[pl.BlockSpec((n, D_MODEL), lambda i, l: (0, 0)),
                   pl.BlockSpec((n, CONV_W - 1, 2 * D_FF), lambda i, l: (0, 0, 0))],
        scratch_shapes=[pltpu.VMEM((n, D_FF), BF16)])
    return pl.pallas_call(_ffn_sample_kernel, grid_spec=grid_spec,
                          out_shape=(jax.ShapeDtypeStruct(x.shape, F32),
                                     jax.ShapeDtypeStruct((n, CONV_W - 1, 2 * D_FF), F32)),
                          compiler_params=_cparams(1), name="ffn_sample")(
                              layer, x, gpre, w_up, conv_w, conv_b, w_down, gpost, buf)


def _s_attn_kernel(l_ref, q_ref, kv_ref, ckt_ref, cvt_ref, sink_ref, o_ref):
    bt = q_ref.shape[0]
    inst = [(bi, g) for bi in range(bt) for g in range(N_KV)]
    hs = [slice(g * GQA_G, (g + 1) * GQA_G) for g in range(N_KV)]
    qs = [q_ref[bi, hs[g], :] for bi, g in inst]
    s = [_dot(q.astype(BF16), ckt_ref[bi, g].astype(BF16)) for q, (bi, g) in zip(qs, inst)]
    s_new = [jnp.sum(q * kv_ref[bi:bi + 1, g * HEAD_DIM:(g + 1) * HEAD_DIM], axis=-1, keepdims=True)
             for q, (bi, g) in zip(qs, inst)]
    sk = [sink_ref[hs[g], :] for g in range(N_KV)]
    m = [jnp.maximum(jnp.maximum(jnp.max(a, axis=-1, keepdims=True), b), sk[g])
         for a, b, (_, g) in zip(s, s_new, inst)]
    p = [jnp.exp(a - mm) for a, mm in zip(s, m)]
    p_new = [jnp.exp(b - mm) for b, mm in zip(s_new, m)]
    den = [jnp.sum(pp, axis=-1, keepdims=True) + pn + jnp.exp(sk[g] - mm)
           for pp, pn, mm, (_, g) in zip(p, p_new, m, inst)]
    for pp, pn, dd, (bi, g) in zip(p, p_new, den, inst):
        v_new = kv_ref[bi:bi + 1, KV_W + g * HEAD_DIM:KV_W + (g + 1) * HEAD_DIM]
        o = _dot_nt(pp.astype(BF16), cvt_ref[bi, g].astype(BF16)) + pn * v_new
        o_ref[bi, hs[g], :] = o / dd


def _s_attn(layer, q, kv, cache_kt, cache_vt, sink_col):
    n = q.shape[0]
    w = cache_kt.shape[-1]
    grid_spec = pltpu.PrefetchScalarGridSpec(
        num_scalar_prefetch=1, grid=(n // S_BT,),
        in_specs=[
            pl.BlockSpec((S_BT, N_HEADS, HEAD_DIM), lambda i, l: (i, 0, 0)),
            pl.BlockSpec((S_BT, 2 * KV_W), lambda i, l: (i, 0)),
            pl.BlockSpec((None, S_BT, N_KV, HEAD_DIM, w), lambda i, l: (l[0], i, 0, 0, 0)),
            pl.BlockSpec((None, S_BT, N_KV, HEAD_DIM, w), lambda i, l: (l[0], i, 0, 0, 0)),
            pl.BlockSpec((N_HEADS, 1), lambda i, l: (0, 0)),
        ],
        out_specs=pl.BlockSpec((S_BT, N_HEADS, HEAD_DIM), lambda i, l: (i, 0, 0)))
    return pl.pallas_call(_s_attn_kernel, grid_spec=grid_spec,
                          out_shape=jax.ShapeDtypeStruct((n, N_HEADS, HEAD_DIM), F32),
                          compiler_params=_cparams(1), name="s_attn")(
                              layer, q.reshape(n, N_HEADS, HEAD_DIM), kv, cache_kt, cache_vt, sink_col
                          ).reshape(n, ATTN_W)


def _s_hgrn_kernel(l_ref, r_ref, rt_ref, st_ref, lbt_ref, gn_ref, alias_ref, o_ref, so_ref):
    del alias_ref
    bt = r_ref.shape[0]
    for h in range(R_HEADS):
        sl = slice(h * R_DK, (h + 1) * R_DK)
        zt = rt_ref[R_KW + h * R_DK:R_KW + (h + 1) * R_DK, :]
        lbc = lbt_ref[sl, :]
        logf, kt = _forget_and_key(zt, lbc)
        f = jnp.exp(logf)
        qr = r_ref[:, sl]
        q16 = (qr * _sigmoid(qr)).astype(BF16)
        vs = slice(2 * R_KW + h * R_DV, 2 * R_KW + (h + 1) * R_DV)
        gs = slice(2 * R_KW + R_VW + h * R_DV, 2 * R_KW + R_VW + (h + 1) * R_DV)
        s_new = [f[:, bi:bi + 1] * st_ref[bi, h] + kt[:, bi:bi + 1] * r_ref[bi:bi + 1, vs]
                 for bi in range(bt)]
        for bi in range(bt):
            so_ref[bi, h] = s_new[bi]
        o = jnp.concatenate([_dot(q16[bi:bi + 1, :], s_new[bi].astype(BF16))
                             for bi in range(bt)], axis=0)
        on = o * lax.rsqrt(jnp.mean(o * o, axis=-1, keepdims=True) + EPS) * gn_ref[:, sl]
        gr = r_ref[:, gs]
        o_ref[:, sl] = (on * (gr * _sigmoid(gr))).astype(o_ref.dtype)


def _s_hgrn(layer, r, state, lb_col, gn, new_states):
    n = r.shape[0]
    rt = r.reshape(n // S_BT, S_BT, R_IN_W).transpose(0, 2, 1)
    state_spec = pl.BlockSpec((None, S_BT, R_HEADS, R_DK, R_DV), lambda i, l: (l[0], i, 0, 0, 0))
    grid_spec = pltpu.PrefetchScalarGridSpec(
        num_scalar_prefetch=1, grid=(n // S_BT,),
        in_specs=[
            pl.BlockSpec((S_BT, R_IN_W), lambda i, l: (i, 0)),
            pl.BlockSpec((None, R_IN_W, S_BT), lambda i, l: (i, 0, 0)),
            state_spec,
            pl.BlockSpec((R_KW, 1), lambda i, l: (0, 0)),
            pl.BlockSpec((1, R_VW), lambda i, l: (0, 0)),
            pl.BlockSpec(memory_space=pl.ANY),
        ],
        out_specs=[pl.BlockSpec((S_BT, R_VW), lambda i, l: (i, 0)), state_spec])
    return pl.pallas_call(_s_hgrn_kernel, grid_spec=grid_spec,
                          out_shape=(jax.ShapeDtypeStruct((n, R_VW), F32),
                                     jax.ShapeDtypeStruct(new_states.shape, F32)),
                          input_output_aliases={6: 1},
                          compiler_params=_cparams(1), name="s_hgrn")(
                              layer, r, rt, state, lb_col, gn, new_states)


def _s_xattn_kernel(l_ref, x_ref, gpre_ref, wq_ref, ck_ref, cv_ref, wo_ref, gpost_ref, o_ref):
    bt, n_mem = ck_ref.shape[0], ck_ref.shape[1]
    n_dt = X_HEAD_DIM // LANES
    ones = jnp.ones((LANES, LANES), BF16)
    x = x_ref[...]
    h = _rms(x, gpre_ref[...]).astype(BF16)
    q = _dot(h, wq_ref[...]) * (X_HEAD_DIM ** -0.5)
    rows = []
    for bi in range(bt):
        qrep = jnp.concatenate(
            [q[bi:bi + 1, hd * X_HEAD_DIM + dt * LANES:hd * X_HEAD_DIM + (dt + 1) * LANES]
             for dt in range(n_dt) for hd in range(X_HEADS)], axis=0)
        prod = (ck_ref[bi] * qrep[None]).reshape(n_mem * X_ROWS, LANES).astype(BF16)
        part = _dot(prod, ones).reshape(n_mem, X_ROWS, LANES)
        s = part + pltpu.roll(part, X_HEADS, 1)
        p = jnp.exp(s - jnp.max(s, axis=0, keepdims=True))
        o = jnp.sum(p * cv_ref[bi], axis=0) / jnp.sum(p, axis=0)
        rows.append(jnp.concatenate([o[dt * X_HEADS + hd:dt * X_HEADS + hd + 1, :]
                                     for hd in range(X_HEADS) for dt in range(n_dt)], axis=1))
    y = _dot(jnp.concatenate(rows, axis=0).astype(BF16), wo_ref[...])
    o_ref[...] = x + _rms(y, gpost_ref[...])


def _s_xattn(layer, x, gpre, w_xq, cache_k, cache_v, w_xo, gpost):
    n = x.shape[0]
    grid_spec = pltpu.PrefetchScalarGridSpec(
        num_scalar_prefetch=1, grid=(n // X_BT,),
        in_specs=[
            pl.BlockSpec((X_BT, D_MODEL), lambda i, l: (i, 0)),
            pl.BlockSpec((1, D_MODEL), lambda i, l: (0, 0)),
            _resident((None, D_MODEL, D_MODEL), lambda i, l: (l[0], 0, 0)),
            pl.BlockSpec((None, X_BT, N_MEM, X_ROWS, LANES), lambda i, l: (l[0], i, 0, 0, 0)),
            pl.BlockSpec((None, X_BT, N_MEM, X_ROWS, LANES), lambda i, l: (l[0], i, 0, 0, 0)),
            _resident((None, D_MODEL, D_MODEL), lambda i, l: (l[0], 0, 0)),
            pl.BlockSpec((1, D_MODEL), lambda i, l: (0, 0)),
        ],
        out_specs=pl.BlockSpec((X_BT, D_MODEL), lambda i, l: (i, 0)))
    return pl.pallas_call(_s_xattn_kernel, grid_spec=grid_spec,
                          out_shape=jax.ShapeDtypeStruct(x.shape, F32),
                          compiler_params=_cparams(1), name="s_xattn")(
                              layer, x, gpre, w_xq, cache_k, cache_v, w_xo, gpost)


def kernel(x_prompt, x_sample, cache_win_k, cache_win_v, cache_mem_k, cache_mem_v, state_hgrn, cache_ffn_conv, mem_prompt, w_in, w_o, attn_sinks, lb_logits, hgrn_norm, w_xq, w_xk, w_xv, w_xo, w_up, conv_w, conv_b, w_down, g_pre_mix, g_post_mix, g_pre_x, g_post_x, g_mem, g_pre_ffn, g_post_ffn):
    batch, seq, _ = x_prompt.shape
    n_s, dec_seq, _ = x_sample.shape
    depth = w_in.shape[0]
    assert dec_seq == 1 and seq % WINDOW == 0 and seq % HC == 0
    assert cache_win_k.shape[2] == WINDOW and n_s % S_BT == 0 and n_s % X_BT == 0
    tm = min(TM, seq)
    assert seq % tm == 0

    w_in_b, w_o_b, w_xq_b, w_xk_b, w_xv_b, w_xo_b, w_up_b, w_down_b = (
        w.astype(BF16) for w in (w_in, w_o, w_xq, w_xk, w_xv, w_xo, w_up, w_down))
    lb, cos_p, sin_p, cos_s, sin_s = _prep(lb_logits, seq, n_s)
    mk5, mv5, mk, mv = _memkv(mem_prompt.reshape(batch * N_MEM, D_MODEL), g_mem.reshape(depth, 1, D_MODEL),
                              w_xk_b, w_xv_b)

    cwk = jnp.transpose(cache_win_k, (0, 1, 3, 4, 2))
    cwv = jnp.transpose(cache_win_v, (0, 1, 3, 4, 2))
    def tile_view(c):
        c = c.reshape(depth, n_s, N_MEM, X_HEADS, X_HEAD_DIM // LANES, LANES)
        return jnp.transpose(c, (0, 1, 2, 4, 3, 5)).reshape(depth, n_s, N_MEM, X_ROWS, LANES)

    cmk, cmv = tile_view(cache_mem_k), tile_view(cache_mem_v)

    xp = x_prompt.reshape(batch * seq, D_MODEL)
    xs = x_sample.reshape(n_s, D_MODEL)
    st_s = jnp.zeros(state_hgrn.shape, F32)
    per_layer = []
    for l in range(depth):
        layer = jnp.full((1,), l, jnp.int32)
        sink, lb_l, gn_l, cb_l = (a[l:l + 1] for a in (attn_sinks, lb, hgrn_norm, conv_b))
        cw_l = conv_w[l]
        gpm, gom, gpx, gox, gpf, gof = (g[l:l + 1] for g in (g_pre_mix, g_post_mix, g_pre_x, g_post_x,
                                                            g_pre_ffn, g_post_ffn))

        q_p, kv_p, r_p = _proj_in(layer, xp, gpm, w_in_b, cos_p, sin_p, tm, BF16)
        attn_p = _swa(q_p, kv_p, sink, seq)
        rec_p, st_p = _hgrn(r_p, lb_l, gn_l, batch, seq)
        xp = _mix_xattn(layer, attn_p, rec_p, w_o_b, xp, gom, gpx, w_xq_b, mk, mv, w_xo_b, gox, seq, tm)

        q_s, kv_s, r_s = _proj_in(layer, xs, gpm, w_in_b, cos_s, sin_s, n_s, F32)
        attn_s = _s_attn(layer, q_s, kv_s, cwk, cwv, sink.reshape(N_HEADS, 1))
        rec_s, st_s = _s_hgrn(layer, r_s, state_hgrn, lb_l.reshape(R_KW, 1), gn_l, st_s)
        xs = _mix_out(layer, attn_s, rec_s, w_o_b, xs, gom, n_s)
        xs = _s_xattn(layer, xs, gpx, w_xq_b, cmk, cmv, w_xo_b, gox)

        xp, tail_p = _ffn(layer, xp, gpf, w_up_b, cw_l, cb_l, w_down_b, gof, seq, tm)
        xs, tail_s = _ffn_sample(layer, xs, gpf, w_up_b, cw_l, cb_l, w_down_b, gof, cache_ffn_conv)

        kv_keep = kv_p.reshape(batch, seq, 2 * KV_W)[:, seq - WINDOW:, :]
        per_layer.append((kv_keep[..., :KV_W], kv_keep[..., KV_W:], kv_s[:, :KV_W], kv_s[:, KV_W:],
                          st_p, tail_p, tail_s))

    wk_p, wv_p, wk_s, wv_s, st_p, tail_p, tail_s = (jnp.stack(a) for a in zip(*per_layer))
    return (xp.reshape(batch, seq, D_MODEL), xs.reshape(n_s, 1, D_MODEL),
            wk_p.reshape(depth, batch, WINDOW, N_KV, HEAD_DIM), wv_p.reshape(depth, batch, WINDOW, N_KV, HEAD_DIM),
            wk_s.reshape(depth, n_s, 1, N_KV, HEAD_DIM), wv_s.reshape(depth, n_s, 1, N_KV, HEAD_DIM),
            mk5.reshape(depth, batch, N_MEM, X_HEADS, X_HEAD_DIM), mv5.reshape(depth, batch, N_MEM, X_HEADS, X_HEAD_DIM),
            st_p, st_s,
            tail_p, tail_s)
```

```python
import functools
import math

import jax
import jax.numpy as jnp
from jax import lax
from jax.experimental import pallas as pl
from jax.experimental.pallas import tpu as pltpu

F32 = jnp.float32
BF16 = jnp.bfloat16

D_MODEL = 1024
HEAD_DIM = 64
N_HEADS = 8
N_KV = 2
GQA_G = N_HEADS // N_KV
WINDOW = 128
ROPE_THETA = 10000.0
R_HEADS = 4
R_DK = 128
R_DV = 128
ATTN_W = N_HEADS * HEAD_DIM
KV_W = N_KV * HEAD_DIM
R_KW = R_HEADS * R_DK
R_VW = R_HEADS * R_DV
R_IN_W = 2 * R_KW + 2 * R_VW
P_IN = ATTN_W + 2 * KV_W + R_IN_W
N_MEM = 256
X_HEADS = 4
X_HEAD_DIM = D_MODEL // X_HEADS
X_ROWS = X_HEADS * (X_HEAD_DIM // 128)
D_FF = 2816
CONV_W = 3
EPS = 1e-6
PAST_LEN = 8192

LANES = 128
SUBLANES = 8
VMEM_LIMIT = 56 * 1024 * 1024

TM = 1024
SWA_TQ = 1024
HC = 128
HG_ROWS = 2048
LOG2_E = 1.4426950408889634
FC = 256
S_BT = 16
X_BT = 8


def _cparams(n_axes):
    return pltpu.CompilerParams(dimension_semantics=("arbitrary",) * n_axes,
                                vmem_limit_bytes=VMEM_LIMIT)


def _rms(x, g):
    return x * lax.rsqrt(jnp.mean(x * x, axis=-1, keepdims=True) + EPS) * g


def _sigmoid(x):
    return 1.0 / (1.0 + jnp.exp(-x))


def _dot(a, b):
    return jnp.dot(a, b, preferred_element_type=F32)


def _dot_nt(a, b):
    return lax.dot_general(a, b, (((1,), (1,)), ((), ())), preferred_element_type=F32)


def _dot_tn(a, b):
    return lax.dot_general(a, b, (((0,), (0,)), ((), ())), preferred_element_type=F32)


def _resident(block_shape, index_map):
    return pl.BlockSpec(block_shape, index_map, pipeline_mode=pl.Buffered(1))


def _prep_kernel(lbl_ref, lb_ref, cos_ref, sin_ref, cos_s_ref, sin_s_ref):
    x = lbl_ref[...]
    e = jnp.exp(x - jnp.max(x, axis=0, keepdims=True))
    sm = e / jnp.sum(e, axis=0, keepdims=True)
    depth = x.shape[0]
    acc = jnp.zeros((1, x.shape[1]), F32)
    rows = [acc]
    for l in range(1, depth):
        acc = acc + sm[l:l + 1, :]
        rows.append(acc)
    lb_ref[...] = jnp.concatenate(rows, axis=0)

    half = HEAD_DIM // 2

    def tables(shape, pos):
        lane = lax.broadcasted_iota(jnp.int32, shape, 1)
        j = (lane & (half - 1)).astype(F32)
        inv_freq = jnp.exp(j * (-math.log(ROPE_THETA) / half))
        ang = pos * inv_freq
        first = (lane & (HEAD_DIM - 1)) < half
        return jnp.cos(ang), jnp.where(first, -jnp.sin(ang), jnp.sin(ang))

    pos_p = lax.broadcasted_iota(jnp.int32, cos_ref.shape, 0).astype(F32)
    c, s = tables(cos_ref.shape, pos_p)
    cos_ref[...] = c
    sin_ref[...] = s
    c, s = tables(cos_s_ref.shape, jnp.full(cos_s_ref.shape, float(PAST_LEN), F32))
    cos_s_ref[...] = c
    sin_s_ref[...] = s


def _prep(lb_logits, seq, n_sample):
    depth = lb_logits.shape[0]
    out_shape = (jax.ShapeDtypeStruct((depth, R_KW), F32),
                 jax.ShapeDtypeStruct((seq, LANES), F32), jax.ShapeDtypeStruct((seq, LANES), F32),
                 jax.ShapeDtypeStruct((n_sample, LANES), F32), jax.ShapeDtypeStruct((n_sample, LANES), F32))
    return pl.pallas_call(_prep_kernel, out_shape=out_shape, name="prep")(lb_logits)


def _rope(x, cos, sin):
    w = x.shape[-1]
    reps = w // LANES
    if reps > 1:
        cos = jnp.tile(cos, (1, reps))
        sin = jnp.tile(sin, (1, reps))
    half = HEAD_DIM // 2
    lane = lax.broadcasted_iota(jnp.int32, x.shape, 1)
    first = (lane & (HEAD_DIM - 1)) < half
    swapped = jnp.where(first, pltpu.roll(x, w - half, 1), pltpu.roll(x, half, 1))
    return x * cos + swapped * sin


def _proj_in_kernel(l_ref, x_ref, g_ref, w_ref, cos_ref, sin_ref, q_ref, kv_ref, r_ref):
    h = _rms(x_ref[...], g_ref[...]).astype(BF16)
    cos = cos_ref[...]
    sin = sin_ref[...]
    q = _dot(h, w_ref[:, 0:ATTN_W])
    q_ref[...] = (_rope(q, cos, sin) * (HEAD_DIM ** -0.5)).astype(q_ref.dtype)
    k = _dot(h, w_ref[:, ATTN_W:ATTN_W + KV_W])
    kv_ref[:, 0:KV_W] = _rope(k, cos, sin)
    kv_ref[:, KV_W:2 * KV_W] = _dot(h, w_ref[:, ATTN_W + KV_W:ATTN_W + 2 * KV_W])
    r_ref[...] = _dot(h, w_ref[:, ATTN_W + 2 * KV_W:P_IN])


def _proj_in(layer, x, g, w_in, cos, sin, tm, q_dtype):
    t = x.shape[0]
    n_tab = cos.shape[0] // tm
    grid_spec = pltpu.PrefetchScalarGridSpec(
        num_scalar_prefetch=1, grid=(t // tm,),
        in_specs=[
            pl.BlockSpec((tm, D_MODEL), lambda i, l: (i, 0)),
            pl.BlockSpec((1, D_MODEL), lambda i, l: (0, 0)),
            _resident((None, D_MODEL, P_IN), lambda i, l: (l[0], 0, 0)),
            pl.BlockSpec((tm, LANES), lambda i, l: (i % n_tab, 0)),
            pl.BlockSpec((tm, LANES), lambda i, l: (i % n_tab, 0)),
        ],
        out_specs=[
            pl.BlockSpec((tm, ATTN_W), lambda i, l: (i, 0)),
            pl.BlockSpec((tm, 2 * KV_W), lambda i, l: (i, 0)),
            pl.BlockSpec((tm, R_IN_W), lambda i, l: (i, 0)),
        ])
    out_shape = (jax.ShapeDtypeStruct((t, ATTN_W), q_dtype),
                 jax.ShapeDtypeStruct((t, 2 * KV_W), F32),
                 jax.ShapeDtypeStruct((t, R_IN_W), F32))
    return pl.pallas_call(_proj_in_kernel, grid_spec=grid_spec, out_shape=out_shape,
                          compiler_params=_cparams(1), name="proj_in")(layer, x, g, w_in, cos, sin)


def _swa_kernel(q_ref, kv_ref, kvp_ref, sink_ref, o_ref, *, tiles_per_batch):
    i = pl.program_id(0)
    w = WINDOW
    nblk = q_ref.shape[0] // w
    first_key = jnp.where(i % tiles_per_batch == 0, w, 0)
    row = lax.broadcasted_iota(jnp.int32, (w, 2 * w), 0)
    col = lax.broadcasted_iota(jnp.int32, (w, 2 * w), 1)
    band = (col >= row) & (col <= row + w)
    low_o = lax.broadcasted_iota(jnp.int32, (w, LANES), 1) < HEAD_DIM
    low_v = lax.broadcasted_iota(jnp.int32, (2 * w, LANES), 1) < HEAD_DIM
    zeros_half = jnp.zeros((HEAD_DIM, 2 * w), BF16)
    valid, k_pad, vcat = [], [], []
    for b in range(nblk):
        if b == 0:
            kv2 = jnp.concatenate([kvp_ref[...], kv_ref[0:w, :]], axis=0)
            valid.append(band & (col >= first_key))
        else:
            kv2 = kv_ref[(b - 1) * w:(b + 1) * w, :]
            valid.append(band)
        kt = kv2[:, 0:KV_W].T.astype(BF16)
        v2 = kv2[:, KV_W:2 * KV_W]
        v_sw = pltpu.roll(v2, HEAD_DIM, 1)
        for g in range(N_KV):
            kg = kt[g * HEAD_DIM:(g + 1) * HEAD_DIM, :]
            k_pad.append((jnp.concatenate([kg, zeros_half], axis=0), jnp.concatenate([zeros_half, kg], axis=0)))
            if g == 0:
                v_lo, v_hi = jnp.where(low_v, v2, 0.0), jnp.where(low_v, 0.0, v_sw)
            else:
                v_lo, v_hi = jnp.where(low_v, v_sw, 0.0), jnp.where(low_v, 0.0, v2)
            vcat.append(jnp.concatenate([v_lo, v_hi], axis=0).astype(BF16))

    inst = [(b, h) for b in range(nblk) for h in range(N_HEADS)]
    scores = [jnp.where(valid[b],
                        _dot(q_ref[b * w:(b + 1) * w, (h // 2) * LANES:(h // 2 + 1) * LANES],
                             k_pad[b * N_KV + h // GQA_G][h % 2]),
                        -jnp.inf) for b, h in inst]
    sinks = [sink_ref[0:1, h:h + 1] for _, h in inst]
    maxes = [jnp.maximum(jnp.max(s, axis=-1, keepdims=True), sk) for s, sk in zip(scores, sinks)]
    probs = [jnp.exp(s - m) for s, m in zip(scores, maxes)]
    rden = [1.0 / (jnp.sum(p, axis=-1, keepdims=True) + jnp.exp(sk - m))
            for p, sk, m in zip(probs, sinks, maxes)]
    for b in range(nblk):
        for pr in range(N_HEADS // 2):
            lo = b * N_HEADS + 2 * pr
            ps = slice(pr * LANES, (pr + 1) * LANES)
            pcat = jnp.concatenate([probs[lo].astype(BF16), probs[lo + 1].astype(BF16)], axis=1)
            o = _dot(pcat, vcat[b * N_KV + (2 * pr) // GQA_G])
            o_ref[b * w:(b + 1) * w, ps] = (o * jnp.where(low_o, rden[lo], rden[lo + 1])).astype(o_ref.dtype)


def _swa(q, kv, sink, seq):
    t = q.shape[0]
    tq = min(SWA_TQ, seq)
    nblk = tq // WINDOW
    body = functools.partial(_swa_kernel, tiles_per_batch=seq // tq)
    return pl.pallas_call(
        body, grid=(t // tq,),
        in_specs=[
            pl.BlockSpec((tq, ATTN_W), lambda i: (i, 0)),
            pl.BlockSpec((tq, 2 * KV_W), lambda i: (i, 0)),
            pl.BlockSpec((WINDOW, 2 * KV_W), lambda i: (jnp.maximum(i * nblk - 1, 0), 0)),
            pl.BlockSpec((1, N_HEADS), lambda i: (0, 0)),
        ],
        out_specs=pl.BlockSpec((tq, ATTN_W), lambda i: (i, 0)),
        out_shape=jax.ShapeDtypeStruct(q.shape, BF16),
        compiler_params=_cparams(1), name="swa")(q, kv, kv, sink)


def _log_forget(z, lb):
    logsig = jnp.minimum(z, 0.0) - jnp.log(1.0 + jnp.exp(-jnp.abs(z)))
    a = jnp.log(lb)
    c = jnp.log(1.0 - lb) + logsig
    return jnp.maximum(a, c) + jnp.log(1.0 + jnp.exp(-jnp.abs(a - c)))


def _cumsum_rows(g, tril):
    hi = g.astype(BF16)
    r1 = g - hi.astype(F32)
    mid = r1.astype(BF16)
    lo = (r1 - mid.astype(F32)).astype(BF16)
    return _dot(tril, hi) + _dot(tril, mid) + _dot(tril, lo)


def _level_ref(b, half):
    n_rows, width = b.shape
    n = 2 * half
    if n >= 2 * SUBLANES:
        pieces = [jnp.broadcast_to(b[i * n + half - 1:i * n + half, :], (n, width))
                  for i in range(n_rows // n)]
        return pieces[0] if len(pieces) == 1 else jnp.concatenate(pieces, axis=0)
    if half == 1:
        odd = (lax.broadcasted_iota(jnp.int32, b.shape, 0) & 1) == 1
        return jnp.where(odd, pltpu.roll(b, 1, 0), b)
    b3 = b.reshape(n_rows // SUBLANES, SUBLANES, width)
    sub = lax.broadcasted_iota(jnp.int32, b3.shape, 1)

    def bcast(r):
        return jnp.broadcast_to(b3[:, r:r + 1, :], b3.shape)

    ref = bcast(3) if half == 4 else jnp.where(sub < 4, bcast(1), bcast(5))
    return ref.reshape(n_rows, width)


def _hgrn_kernel(qr_ref, fr_ref, ir_ref, gr_ref, lb_ref, gn_ref, o_ref, s_out_ref, st_ref):
    c = pl.program_id(1)
    last = pl.num_programs(1) - 1

    @pl.when(c == 0)
    def _():
        st_ref[...] = jnp.zeros_like(st_ref)

    rr = lax.broadcasted_iota(jnp.int32, (HC, HC), 0)
    cc = lax.broadcasted_iota(jnp.int32, (HC, HC), 1)
    xor = rr ^ cc
    causal = cc <= rr
    tril = jnp.where(causal, 1.0, 0.0).astype(BF16)
    halves = [1 << i for i in range(HC.bit_length() - 1)]
    row_w = lax.broadcasted_iota(jnp.int32, (HC, R_KW), 0)
    signs = [jnp.where((row_w & half) != 0, 1.0, -1.0) for half in halves]
    heads = [slice(h * R_DK, (h + 1) * R_DK) for h in range(R_HEADS)]
    st = [st_ref[h] for h in range(R_HEADS)]

    for ci in range(qr_ref.shape[0] // HC):
        rows = slice(ci * HC, (ci + 1) * HC)
        g = _log_forget(fr_ref[rows, :], lb_ref[...]) * LOG2_E
        k = 1.0 - jnp.exp2(g)
        qr = qr_ref[rows, :]
        q = qr * _sigmoid(qr)
        v16 = ir_ref[rows, :].astype(BF16)
        b = _cumsum_rows(g, tril)
        q16 = q.astype(BF16)
        k16 = k.astype(BF16)
        b_last = b[HC - 1:HC, :]
        q_in = (q * jnp.exp2(b)).astype(BF16)
        k_out = (k * jnp.exp2(b_last - b)).astype(BF16)
        decay = jnp.exp2(b_last)
        gr = gr_ref[rows, :]
        gate = gr * _sigmoid(gr)
        q_lv, k_lv = [], []
        for half, sgn in zip(halves, signs):
            e = jnp.exp2((b - _level_ref(b, half)) * sgn).astype(BF16)
            q_lv.append(q16 * e)
            k_lv.append(k16 * e)

        diag = [_dot_nt(q16[:, sl], k16[:, sl]) for sl in heads]
        levels = [[_dot_nt(ql[:, sl], kl[:, sl]) for sl in heads] for ql, kl in zip(q_lv, k_lv)]
        att16 = []
        for h in range(R_HEADS):
            att = diag[h]
            for half, lv in zip(halves, levels):
                att = jnp.where(xor >= half, lv[h], att)
            att16.append(jnp.where(causal, att, 0.0).astype(BF16))
        outs = [_dot(att16[h], v16[:, sl]) + _dot_nt(q_in[:, sl], st[h].astype(BF16))
                for h, sl in enumerate(heads)]
        st = [st[h] * decay[:, sl] + _dot_tn(v16[:, sl], k_out[:, sl]) for h, sl in enumerate(heads)]
        for h, sl in enumerate(heads):
            o = outs[h]
            on = o * lax.rsqrt(jnp.mean(o * o, axis=-1, keepdims=True) + EPS) * gn_ref[:, sl]
            o_ref[rows, sl] = (on * gate[:, sl]).astype(o_ref.dtype)

    for h in range(R_HEADS):
        st_ref[h] = st[h]

    @pl.when(c == last)
    def _():
        for h in range(R_HEADS):
            s_out_ref[0, h] = st[h].T


def _hgrn(r, lb, gn, batch, seq):
    rows = min(HG_ROWS, seq)
    nc = seq // rows
    t = r.shape[0]

    def col(kk):
        return pl.BlockSpec((rows, R_KW), lambda b, c: (b * nc + c, kk))

    return pl.pallas_call(
        _hgrn_kernel, grid=(batch, nc),
        in_specs=[col(0), col(1), col(2), col(3),
                  pl.BlockSpec((1, R_KW), lambda b, c: (0, 0)),
                  pl.BlockSpec((1, R_VW), lambda b, c: (0, 0))],
        out_specs=[pl.BlockSpec((rows, R_VW), lambda b, c: (b * nc + c, 0)),
                   pl.BlockSpec((1, R_HEADS, R_DK, R_DV), lambda b, c: (b, 0, 0, 0))],
        out_shape=(jax.ShapeDtypeStruct((t, R_VW), BF16),
                   jax.ShapeDtypeStruct((batch, R_HEADS, R_DK, R_DV), F32)),
        scratch_shapes=[pltpu.VMEM((R_HEADS, R_DV, R_DK), F32)],
        compiler_params=_cparams(2), name="hgrn")(r, r, r, r, lb, gn)


def _mix_out_kernel(l_ref, a_ref, r_ref, w_ref, x_ref, g_ref, o_ref):
    m = (_dot(a_ref[...].astype(BF16), w_ref[0:ATTN_W, :])
         + _dot(r_ref[...].astype(BF16), w_ref[ATTN_W:ATTN_W + R_VW, :]))
    o_ref[...] = x_ref[...] + _rms(m, g_ref[...])


def _mix_out(layer, attn, rec, w_o, x, g, tm):
    t = x.shape[0]
    grid_spec = pltpu.PrefetchScalarGridSpec(
        num_scalar_prefetch=1, grid=(t // tm,),
        in_specs=[
            pl.BlockSpec((tm, ATTN_W), lambda i, l: (i, 0)),
            pl.BlockSpec((tm, R_VW), lambda i, l: (i, 0)),
            _resident((None, ATTN_W + R_VW, D_MODEL), lambda i, l: (l[0], 0, 0)),
            pl.BlockSpec((tm, D_MODEL), lambda i, l: (i, 0)),
            pl.BlockSpec((1, D_MODEL), lambda i, l: (0, 0)),
        ],
        out_specs=pl.BlockSpec((tm, D_MODEL), lambda i, l: (i, 0)))
    return pl.pallas_call(_mix_out_kernel, grid_spec=grid_spec,
                          out_shape=jax.ShapeDtypeStruct(x.shape, F32),
                          compiler_params=_cparams(1), name="mix_out")(layer, attn, rec, w_o, x, g)


def _memkv_kernel(m_ref, g_ref, wk_ref, wv_ref, k_ref, v_ref, k16_ref, v16_ref):
    h = _rms(m_ref[...], g_ref[...]).astype(BF16)
    for w_ref, o_ref, o16_ref in ((wk_ref, k_ref, k16_ref), (wv_ref, v_ref, v16_ref)):
        y = _dot(h, w_ref[...])
        o16_ref[...] = y.astype(BF16)
        for hd in range(X_HEADS):
            o_ref[:, hd, :] = y[:, hd * X_HEAD_DIM:(hd + 1) * X_HEAD_DIM]


def _memkv(mem, g_mem, w_xk, w_xv):
    depth = w_xk.shape[0]
    rows = mem.shape[0]
    tm = min(TM, rows)
    out = jax.ShapeDtypeStruct((depth, rows, X_HEADS, X_HEAD_DIM), F32)
    out16 = jax.ShapeDtypeStruct((depth, rows, D_MODEL), BF16)
    return pl.pallas_call(
        _memkv_kernel, grid=(depth, rows // tm),
        in_specs=[
            pl.BlockSpec((tm, D_MODEL), lambda l, i: (i, 0)),
            pl.BlockSpec((None, 1, D_MODEL), lambda l, i: (l, 0, 0)),
            pl.BlockSpec((None, D_MODEL, D_MODEL), lambda l, i: (l, 0, 0)),
            pl.BlockSpec((None, D_MODEL, D_MODEL), lambda l, i: (l, 0, 0)),
        ],
        out_specs=[pl.BlockSpec((None, tm, X_HEADS, X_HEAD_DIM), lambda l, i: (l, i, 0, 0)),
                   pl.BlockSpec((None, tm, X_HEADS, X_HEAD_DIM), lambda l, i: (l, i, 0, 0)),
                   pl.BlockSpec((None, tm, D_MODEL), lambda l, i: (l, i, 0)),
                   pl.BlockSpec((None, tm, D_MODEL), lambda l, i: (l, i, 0))],
        out_shape=(out, out, out16, out16), compiler_params=_cparams(2), name="memkv")(mem, g_mem, w_xk, w_xv)


def _mix_xattn_kernel(l_ref, a_ref, r_ref, wmix_ref, x_ref, gmix_ref,
                      gpre_ref, wq_ref, mk_ref, mv_ref, wo_ref, gpost_ref, o_ref):
    half = x_ref.shape[0] // 2
    parts = [slice(i * half, (i + 1) * half) for i in range(2)]
    heads = [slice(hd * X_HEAD_DIM, (hd + 1) * X_HEAD_DIM) for hd in range(X_HEADS)]
    m = [_dot(a_ref[rs, :], wmix_ref[0:ATTN_W, :]) + _dot(r_ref[rs, :], wmix_ref[ATTN_W:ATTN_W + R_VW, :])
         for rs in parts]
    x = [x_ref[rs, :] + _rms(mm, gmix_ref[...]) for rs, mm in zip(parts, m)]
    q = [(_dot(_rms(xx, gpre_ref[...]).astype(BF16), wq_ref[...]) * (X_HEAD_DIM ** -0.5)).astype(BF16)
         for xx in x]
    inst = [(pi, sl) for pi in range(2) for sl in heads]
    s = [_dot_nt(q[pi][:, sl], mk_ref[:, sl]) for pi, sl in inst]
    p = [jnp.exp(a - jnp.max(a, axis=-1, keepdims=True)) for a in s]
    den = [jnp.sum(a, axis=-1, keepdims=True) for a in p]
    outs = [(_dot(a.astype(BF16), mv_ref[:, sl]) / d).astype(BF16) for a, d, (_, sl) in zip(p, den, inst)]
    for pi, rs in enumerate(parts):
        y = _dot(jnp.concatenate(outs[pi * X_HEADS:(pi + 1) * X_HEADS], axis=1), wo_ref[...])
        o_ref[rs, :] = x[pi] + _rms(y, gpost_ref[...])


def _mix_xattn(layer, attn, rec, w_o, x, gmix, gpre, w_xq, mk, mv, w_xo, gpost, seq, tm):
    t = x.shape[0]
    tpb = seq // tm
    grid_spec = pltpu.PrefetchScalarGridSpec(
        num_scalar_prefetch=1, grid=(t // tm,),
        in_specs=[
            pl.BlockSpec((tm, ATTN_W), lambda i, l: (i, 0)),
            pl.BlockSpec((tm, R_VW), lambda i, l: (i, 0)),
            _resident((None, ATTN_W + R_VW, D_MODEL), lambda i, l: (l[0], 0, 0)),
            pl.BlockSpec((tm, D_MODEL), lambda i, l: (i, 0)),
            pl.BlockSpec((1, D_MODEL), lambda i, l: (0, 0)),
            pl.BlockSpec((1, D_MODEL), lambda i, l: (0, 0)),
            _resident((None, D_MODEL, D_MODEL), lambda i, l: (l[0], 0, 0)),
            pl.BlockSpec((None, N_MEM, D_MODEL), lambda i, l: (l[0], i // tpb, 0)),
            pl.BlockSpec((None, N_MEM, D_MODEL), lambda i, l: (l[0], i // tpb, 0)),
            _resident((None, D_MODEL, D_MODEL), lambda i, l: (l[0], 0, 0)),
            pl.BlockSpec((1, D_MODEL), lambda i, l: (0, 0)),
        ],
        out_specs=pl.BlockSpec((tm, D_MODEL), lambda i, l: (i, 0)))
    return pl.pallas_call(_mix_xattn_kernel, grid_spec=grid_spec,
                          out_shape=jax.ShapeDtypeStruct(x.shape, F32),
                          compiler_params=_cparams(1), name="mix_xattn")(
                              layer, attn, rec, w_o, x, gmix, gpre, w_xq, mk, mv, w_xo, gpost)


def _conv_chunk(u, cols, cw_ref, cb_ref, carry_ref, sub0):
    tm = u.shape[0]
    w0 = cw_ref[0:1, cols]
    w1 = cw_ref[1:2, cols]
    w2 = cw_ref[2:3, cols]

    def shift1(a, first):
        rolled = pltpu.roll(a, 1, 0)
        head = jnp.where(sub0, first, rolled[0:SUBLANES])
        return jnp.concatenate([head, rolled[SUBLANES:]], axis=0)

    p0 = carry_ref[0:1, cols]
    p1 = carry_ref[1:2, cols]
    carry_ref[:, cols] = u[tm - (CONV_W - 1):tm, :]
    v = u * w1 + shift1(u * w0, p1 * w0)
    return cb_ref[:, cols] + u * w2 + shift1(v, p1 * w1 + p0 * w0)


def _ffn_kernel(l_ref, x_ref, gpre_ref, wup_ref, cw_ref, cb_ref, wdn_ref, gpost_ref,
                o_ref, tail_ref, h_ref, act_ref, carry_ref, *, tiles_per_batch):
    i = pl.program_id(0)

    @pl.when(i % tiles_per_batch == 0)
    def _():
        carry_ref[...] = jnp.zeros_like(carry_ref)

    h_ref[...] = _rms(x_ref[...], gpre_ref[...]).astype(BF16)
    sub0 = lax.broadcasted_iota(jnp.int32, (SUBLANES, FC), 0) == 0
    for j in range(D_FF // FC):
        ca_cols = slice(j * FC, (j + 1) * FC)
        cb_cols = slice(D_FF + j * FC, D_FF + (j + 1) * FC)
        ca = _conv_chunk(_dot(h_ref[...], wup_ref[:, ca_cols]), ca_cols, cw_ref, cb_ref, carry_ref, sub0)
        cb = _conv_chunk(_dot(h_ref[...], wup_ref[:, cb_cols]), cb_cols, cw_ref, cb_ref, carry_ref, sub0)
        act_ref[:, ca_cols] = (ca * _sigmoid(ca) * cb).astype(BF16)
    o_ref[...] = x_ref[...] + _rms(_dot(act_ref[...], wdn_ref[...]), gpost_ref[...])

    @pl.when(i % tiles_per_batch == tiles_per_batch - 1)
    def _():
        tail_ref[0] = carry_ref[...]


def _ffn(layer, x, gpre, w_up, conv_w, conv_b, w_down, gpost, seq, tm):
    t = x.shape[0]
    tiles_per_batch = seq // tm
    grid_spec = pltpu.PrefetchScalarGridSpec(
        num_scalar_prefetch=1, grid=(t // tm,),
        in_specs=[
            pl.BlockSpec((tm, D_MODEL), lambda i, l: (i, 0)),
            pl.BlockSpec((1, D_MODEL), lambda i, l: (0, 0)),
            _resident((None, D_MODEL, 2 * D_FF), lambda i, l: (l[0], 0, 0)),
            pl.BlockSpec((CONV_W, 2 * D_FF), lambda i, l: (0, 0)),
            pl.BlockSpec((1, 2 * D_FF), lambda i, l: (0, 0)),
            _resident((None, D_FF, D_MODEL), lambda i, l: (l[0], 0, 0)),
            pl.BlockSpec((1, D_MODEL), lambda i, l: (0, 0)),
        ],
        out_specs=[
            pl.BlockSpec((tm, D_MODEL), lambda i, l: (i, 0)),
            pl.BlockSpec((1, CONV_W - 1, 2 * D_FF), lambda i, l: (i // tiles_per_batch, 0, 0)),
        ],
        scratch_shapes=[pltpu.VMEM((tm, D_MODEL), BF16), pltpu.VMEM((tm, D_FF), BF16),
                        pltpu.VMEM((CONV_W - 1, 2 * D_FF), F32)])
    body = functools.partial(_ffn_kernel, tiles_per_batch=tiles_per_batch)
    return pl.pallas_call(body, grid_spec=grid_spec,
                          out_shape=(jax.ShapeDtypeStruct(x.shape, F32),
                                     jax.ShapeDtypeStruct((t // seq, CONV_W - 1, 2 * D_FF), F32)),
                          compiler_params=_cparams(1), name="ffn")(
                              layer, x, gpre, w_up, conv_w, conv_b, w_down, gpost)


def _ffn_sample_kernel(l_ref, x_ref, gpre_ref, wup_ref, cw_ref, cb_ref, wdn_ref, gpost_ref, buf_ref,
                       o_ref, tail_ref, act_ref):
    x = x_ref[...]
    h = _rms(x, gpre_ref[...]).astype(BF16)

    def conv(u, cols):
        u2 = buf_ref[:, 0, cols]
        u1 = buf_ref[:, 1, cols]
        tail_ref[:, 0, cols] = u1
        tail_ref[:, 1, cols] = u
        return cb_ref[:, cols] + u2 * cw_ref[0:1, cols] + u1 * cw_ref[1:2, cols] + u * cw_ref[2:3, cols]

    for j in range(D_FF // FC):
        ca_cols = slice(j * FC, (j + 1) * FC)
        cb_cols = slice(D_FF + j * FC, D_FF + (j + 1) * FC)
        ca = conv(_dot(h, wup_ref[:, ca_cols]), ca_cols)
        cb = conv(_dot(h, wup_ref[:, cb_cols]), cb_cols)
        act_ref[:, ca_cols] = (ca * _sigmoid(ca) * cb).astype(BF16)
    o_ref[...] = x + _rms(_dot(act_ref[...], wdn_ref[...]), gpost_ref[...])


def _ffn_sample(layer, x, gpre, w_up, conv_w, conv_b, w_down, gpost, buf):
    n = x.shape[0]
    grid_spec = pltpu.PrefetchScalarGridSpec(
        num_scalar_prefetch=1, grid=(1,),
        in_specs=[
            pl.BlockSpec((n, D_MODEL), lambda i, l: (0, 0)),
            pl.BlockSpec((1, D_MODEL), lambda i, l: (0, 0)),
            _resident((None, D_MODEL, 2 * D_FF), lambda i, l: (l[0], 0, 0)),
            pl.BlockSpec((CONV_W, 2 * D_FF), lambda i, l: (0, 0)),
            pl.BlockSpec((1, 2 * D_FF), lambda i, l: (0, 0)),
            _resident((None, D_FF, D_MODEL), lambda i, l: (l[0], 0, 0)),
            pl.BlockSpec((1, D_MODEL), lambda i, l: (0, 0)),
            pl.BlockSpec((None, n, CONV_W - 1, 2 * D_FF), lambda i, l: (l[0], 0, 0, 0)),
        ],
        out_specs=[pl.BlockSpec((n, D_MODEL), lambda i, l: (0, 0)),
                   pl.BlockSpec((n, CONV_W - 1, 2 * D_FF), lambda i, l: (0, 0, 0))],
        scratch_shapes=[pltpu.VMEM((n, D_FF), BF16)])
    return pl.pallas_call(_ffn_sample_kernel, grid_spec=grid_spec,
                          out_shape=(jax.ShapeDtypeStruct(x.shape, F32),
                                     jax.ShapeDtypeStruct((n, CONV_W - 1, 2 * D_FF), F32)),
                          compiler_params=_cparams(1), name="ffn_sample")(
                              layer, x, gpre, w_up, conv_w, conv_b, w_down, gpost, buf)


def _s_attn_kernel(l_ref, q_ref, kv_ref, ckt_ref, cvt_ref, sink_ref, o_ref):
    bt = q_ref.shape[0]
    inst = [(bi, g) for bi in range(bt) for g in range(N_KV)]
    hs = [slice(g * GQA_G, (g + 1) * GQA_G) for g in range(N_KV)]
    qs = [q_ref[bi, hs[g], :] for bi, g in inst]
    s = [_dot(q.astype(BF16), ckt_ref[bi, g].astype(BF16)) for q, (bi, g) in zip(qs, inst)]
    s_new = [jnp.sum(q * kv_ref[bi:bi + 1, g * HEAD_DIM:(g + 1) * HEAD_DIM], axis=-1, keepdims=True)
             for q, (bi, g) in zip(qs, inst)]
    sk = [sink_ref[hs[g], :] for g in range(N_KV)]
    m = [jnp.maximum(jnp.maximum(jnp.max(a, axis=-1, keepdims=True), b), sk[g])
         for a, b, (_, g) in zip(s, s_new, inst)]
    p = [jnp.exp(a - mm) for a, mm in zip(s, m)]
    p_new = [jnp.exp(b - mm) for b, mm in zip(s_new, m)]
    den = [jnp.sum(pp, axis=-1, keepdims=True) + pn + jnp.exp(sk[g] - mm)
           for pp, pn, mm, (_, g) in zip(p, p_new, m, inst)]
    for pp, pn, dd, (bi, g) in zip(p, p_new, den, inst):
        v_new = kv_ref[bi:bi + 1, KV_W + g * HEAD_DIM:KV_W + (g + 1) * HEAD_DIM]
        o = _dot_nt(pp.astype(BF16), cvt_ref[bi, g].astype(BF16)) + pn * v_new
        o_ref[bi, hs[g], :] = o / dd


def _s_attn(layer, q, kv, cache_kt, cache_vt, sink_col):
    n = q.shape[0]
    w = cache_kt.shape[-1]
    grid_spec = pltpu.PrefetchScalarGridSpec(
        num_scalar_prefetch=1, grid=(n // S_BT,),
        in_specs=[
            pl.BlockSpec((S_BT, N_HEADS, HEAD_DIM), lambda i, l: (i, 0, 0)),
            pl.BlockSpec((S_BT, 2 * KV_W), lambda i, l: (i, 0)),
            pl.BlockSpec((None, S_BT, N_KV, HEAD_DIM, w), lambda i, l: (l[0], i, 0, 0, 0)),
            pl.BlockSpec((None, S_BT, N_KV, HEAD_DIM, w), lambda i, l: (l[0], i, 0, 0, 0)),
            pl.BlockSpec((N_HEADS, 1), lambda i, l: (0, 0)),
        ],
        out_specs=pl.BlockSpec((S_BT, N_HEADS, HEAD_DIM), lambda i, l: (i, 0, 0)))
    return pl.pallas_call(_s_attn_kernel, grid_spec=grid_spec,
                          out_shape=jax.ShapeDtypeStruct((n, N_HEADS, HEAD_DIM), F32),
                          compiler_params=_cparams(1), name="s_attn")(
                              layer, q.reshape(n, N_HEADS, HEAD_DIM), kv, cache_kt, cache_vt, sink_col
                          ).reshape(n, ATTN_W)


def _s_hgrn_kernel(l_ref, r_ref, rt_ref, st_ref, lbt_ref, gn_ref, alias_ref, o_ref, so_ref):
    del alias_ref
    bt = r_ref.shape[0]
    for h in range(R_HEADS):
        sl = slice(h * R_DK, (h + 1) * R_DK)
        zt = rt_ref[R_KW + h * R_DK:R_KW + (h + 1) * R_DK, :]
        lbc = lbt_ref[sl, :]
        f = jnp.exp(_log_forget(zt, lbc))
        kt = 1.0 - f
        qr = r_ref[:, sl]
        q16 = (qr * _sigmoid(qr)).astype(BF16)
        vs = slice(2 * R_KW + h * R_DV, 2 * R_KW + (h + 1) * R_DV)
        gs = slice(2 * R_KW + R_VW + h * R_DV, 2 * R_KW + R_VW + (h + 1) * R_DV)
        s_new = [f[:, bi:bi + 1] * st_ref[bi, h] + kt[:, bi:bi + 1] * r_ref[bi:bi + 1, vs]
                 for bi in range(bt)]
        for bi in range(bt):
            so_ref[bi, h] = s_new[bi]
        o = jnp.concatenate([_dot(q16[bi:bi + 1, :], s_new[bi].astype(BF16))
                             for bi in range(bt)], axis=0)
        on = o * lax.rsqrt(jnp.mean(o * o, axis=-1, keepdims=True) + EPS) * gn_ref[:, sl]
        gr = r_ref[:, gs]
        o_ref[:, sl] = (on * (gr * _sigmoid(gr))).astype(o_ref.dtype)


def _s_hgrn(layer, r, state, lb_col, gn, new_states):
    n = r.shape[0]
    rt = r.reshape(n // S_BT, S_BT, R_IN_W).transpose(0, 2, 1)
    state_spec = pl.BlockSpec((None, S_BT, R_HEADS, R_DK, R_DV), lambda i, l: (l[0], i, 0, 0, 0))
    grid_spec = pltpu.PrefetchScalarGridSpec(
        num_scalar_prefetch=1, grid=(n // S_BT,),
        in_specs=[
            pl.BlockSpec((S_BT, R_IN_W), lambda i, l: (i, 0)),
            pl.BlockSpec((None, R_IN_W, S_BT), lambda i, l: (i, 0, 0)),
            state_spec,
            pl.BlockSpec((R_KW, 1), lambda i, l: (0, 0)),
            pl.BlockSpec((1, R_VW), lambda i, l: (0, 0)),
            pl.BlockSpec(memory_space=pl.ANY),
        ],
        out_specs=[pl.BlockSpec((S_BT, R_VW), lambda i, l: (i, 0)), state_spec])
    return pl.pallas_call(_s_hgrn_kernel, grid_spec=grid_spec,
                          out_shape=(jax.ShapeDtypeStruct((n, R_VW), F32),
                                     jax.ShapeDtypeStruct(new_states.shape, F32)),
                          input_output_aliases={6: 1},
                          compiler_params=_cparams(1), name="s_hgrn")(
                              layer, r, rt, state, lb_col, gn, new_states)


def _s_xattn_kernel(l_ref, x_ref, gpre_ref, wq_ref, ck_ref, cv_ref, wo_ref, gpost_ref, o_ref):
    bt, n_mem = ck_ref.shape[0], ck_ref.shape[1]
    n_dt = X_HEAD_DIM // LANES
    ones = jnp.ones((LANES, LANES), BF16)
    x = x_ref[...]
    h = _rms(x, gpre_ref[...]).astype(BF16)
    q = _dot(h, wq_ref[...]) * (X_HEAD_DIM ** -0.5)
    rows = []
    for bi in range(bt):
        qrep = jnp.concatenate(
            [q[bi:bi + 1, hd * X_HEAD_DIM + dt * LANES:hd * X_HEAD_DIM + (dt + 1) * LANES]
             for dt in range(n_dt) for hd in range(X_HEADS)], axis=0)
        prod = (ck_ref[bi] * qrep[None]).reshape(n_mem * X_ROWS, LANES).astype(BF16)
        part = _dot(prod, ones).reshape(n_mem, X_ROWS, LANES)
        s = part + pltpu.roll(part, X_HEADS, 1)
        p = jnp.exp(s - jnp.max(s, axis=0, keepdims=True))
        o = jnp.sum(p * cv_ref[bi], axis=0) / jnp.sum(p, axis=0)
        rows.append(jnp.concatenate([o[dt * X_HEADS + hd:dt * X_HEADS + hd + 1, :]
                                     for hd in range(X_HEADS) for dt in range(n_dt)], axis=1))
    y = _dot(jnp.concatenate(rows, axis=0).astype(BF16), wo_ref[...])
    o_ref[...] = x + _rms(y, gpost_ref[...])


def _s_xattn(layer, x, gpre, w_xq, cache_k, cache_v, w_xo, gpost):
    n = x.shape[0]
    grid_spec = pltpu.PrefetchScalarGridSpec(
        num_scalar_prefetch=1, grid=(n // X_BT,),
        in_specs=[
            pl.BlockSpec((X_BT, D_MODEL), lambda i, l: (i, 0)),
            pl.BlockSpec((1, D_MODEL), lambda i, l: (0, 0)),
            _resident((None, D_MODEL, D_MODEL), lambda i, l: (l[0], 0, 0)),
            pl.BlockSpec((None, X_BT, N_MEM, X_ROWS, LANES), lambda i, l: (l[0], i, 0, 0, 0)),
            pl.BlockSpec((None, X_BT, N_MEM, X_ROWS, LANES), lambda i, l: (l[0], i, 0, 0, 0)),
            _resident((None, D_MODEL, D_MODEL), lambda i, l: (l[0], 0, 0)),
            pl.BlockSpec((1, D_MODEL), lambda i, l: (0, 0)),
        ],
        out_specs=pl.BlockSpec((X_BT, D_MODEL), lambda i, l: (i, 0)))
    return pl.pallas_call(_s_xattn_kernel, grid_spec=grid_spec,
                          out_shape=jax.ShapeDtypeStruct(x.shape, F32),
                          compiler_params=_cparams(1), name="s_xattn")(
                              layer, x, gpre, w_xq, cache_k, cache_v, w_xo, gpost)


def kernel(x_prompt, x_sample, cache_win_k, cache_win_v, cache_mem_k, cache_mem_v, state_hgrn, cache_ffn_conv, mem_prompt, w_in, w_o, attn_sinks, lb_logits, hgrn_norm, w_xq, w_xk, w_xv, w_xo, w_up, conv_w, conv_b, w_down, g_pre_mix, g_post_mix, g_pre_x, g_post_x, g_mem, g_pre_ffn, g_post_ffn):
    batch, seq, _ = x_prompt.shape
    n_s, dec_seq, _ = x_sample.shape
    depth = w_in.shape[0]
    assert dec_seq == 1 and seq % WINDOW == 0 and seq % HC == 0
    assert cache_win_k.shape[2] == WINDOW and n_s % S_BT == 0 and n_s % X_BT == 0
    tm = min(TM, seq)
    assert seq % tm == 0

    w_in_b, w_o_b, w_xq_b, w_xk_b, w_xv_b, w_xo_b, w_up_b, w_down_b = (
        w.astype(BF16) for w in (w_in, w_o, w_xq, w_xk, w_xv, w_xo, w_up, w_down))
    lb, cos_p, sin_p, cos_s, sin_s = _prep(lb_logits, seq, n_s)
    mk5, mv5, mk, mv = _memkv(mem_prompt.reshape(batch * N_MEM, D_MODEL), g_mem.reshape(depth, 1, D_MODEL),
                              w_xk_b, w_xv_b)

    cwk = jnp.transpose(cache_win_k, (0, 1, 3, 4, 2))
    cwv = jnp.transpose(cache_win_v, (0, 1, 3, 4, 2))
    def tile_view(c):
        c = c.reshape(depth, n_s, N_MEM, X_HEADS, X_HEAD_DIM // LANES, LANES)
        return jnp.transpose(c, (0, 1, 2, 4, 3, 5)).reshape(depth, n_s, N_MEM, X_ROWS, LANES)

    cmk, cmv = tile_view(cache_mem_k), tile_view(cache_mem_v)

    xp = x_prompt.reshape(batch * seq, D_MODEL)
    xs = x_sample.reshape(n_s, D_MODEL)
    st_s = jnp.zeros(state_hgrn.shape, F32)
    per_layer = []
    for l in range(depth):
        layer = jnp.full((1,), l, jnp.int32)
        sink, lb_l, gn_l, cb_l = (a[l:l + 1] for a in (attn_sinks, lb, hgrn_norm, conv_b))
        cw_l = conv_w[l]
        gpm, gom, gpx, gox, gpf, gof = (g[l:l + 1] for g in (g_pre_mix, g_post_mix, g_pre_x, g_post_x,
                                                            g_pre_ffn, g_post_ffn))

        q_p, kv_p, r_p = _proj_in(layer, xp, gpm, w_in_b, cos_p, sin_p, tm, BF16)
        attn_p = _swa(q_p, kv_p, sink, seq)
        rec_p, st_p = _hgrn(r_p, lb_l, gn_l, batch, seq)
        xp = _mix_xattn(layer, attn_p, rec_p, w_o_b, xp, gom, gpx, w_xq_b, mk, mv, w_xo_b, gox, seq, tm)

        q_s, kv_s, r_s = _proj_in(layer, xs, gpm, w_in_b, cos_s, sin_s, n_s, F32)
        attn_s = _s_attn(layer, q_s, kv_s, cwk, cwv, sink.reshape(N_HEADS, 1))
        rec_s, st_s = _s_hgrn(layer, r_s, state_hgrn, lb_l.reshape(R_KW, 1), gn_l, st_s)
        xs = _mix_out(layer, attn_s, rec_s, w_o_b, xs, gom, n_s)
        xs = _s_xattn(layer, xs, gpx, w_xq_b, cmk, cmv, w_xo_b, gox)

        xp, tail_p = _ffn(layer, xp, gpf, w_up_b, cw_l, cb_l, w_down_b, gof, seq, tm)
        xs, tail_s = _ffn_sample(layer, xs, gpf, w_up_b, cw_l, cb_l, w_down_b, gof, cache_ffn_conv)

        kv_keep = kv_p.reshape(batch, seq, 2 * KV_W)[:, seq - WINDOW:, :]
        per_layer.append((kv_keep[..., :KV_W], kv_keep[..., KV_W:], kv_s[:, :KV_W], kv_s[:, KV_W:],
                          st_p, tail_p, tail_s))

    wk_p, wv_p, wk_s, wv_s, st_p, tail_p, tail_s = (jnp.stack(a) for a in zip(*per_layer))
    return (xp.reshape(batch, seq, D_MODEL), xs.reshape(n_s, 1, D_MODEL),
            wk_p.reshape(depth, batch, WINDOW, N_KV, HEAD_DIM), wv_p.reshape(depth, batch, WINDOW, N_KV, HEAD_DIM),
            wk_s.reshape(depth, n_s, 1, N_KV, HEAD_DIM), wv_s.reshape(depth, n_s, 1, N_KV, HEAD_DIM),
            mk5.reshape(depth, batch, N_MEM, X_HEADS, X_HEAD_DIM), mv5.reshape(depth, batch, N_MEM, X_HEADS, X_HEAD_DIM),
            st_p, st_s,
            tail_p, tail_s)
```
